```python
import jax, jax.numpy as jnp
from jax import lax
import numpy as np

D_MODEL = 1024
BATCH = 32
SEQ = 2048
DEPTH = 2
DEC_BATCH = 4
DEC_SEQ = 8192
PAST_LEN = 128

PLE_DIM = 256
N_EVEN = (DEPTH + 1) // 2
N_ODD = DEPTH // 2
MLA_HEADS = 8
MLA_Q_LORA = 384
MLA_KV_LORA = 256
MLA_NOPE = 64
MLA_ROPE = 32
MLA_V = 64
MLA_THETA = 10000.0
MLA_Q_BLOCK = 128
DIL_HEADS = 8
DIL_HD = 64
DIL_PATTERNS = ((128, 1), (512, 4), (2048, 16))
DIL_BLOCK = 64
C_HEADS = 16
C_KV_HEADS = 4
C_HD = 64
C_RADIUS = 128
C_BLOCK = 128
ROPE_THETA = 500000.0
ROPE_FRACTION = 4
PEER_HEADS = 8
PEER_N_KEYS = 128
PEER_N_EXPERTS = PEER_N_KEYS * PEER_N_KEYS
PEER_D_KEY = 128
PEER_TOPK = 16
PEER_TOKEN_BLOCK = 128
DN_ALPHA = (2 * DEPTH) ** 0.25
DN_BETA = (8 * DEPTH) ** -0.25
LN_EPS = 1e-5
RMS_EPS = 1e-6
NEG = -1e30

IN_A = MLA_Q_LORA + MLA_KV_LORA + MLA_ROPE + 3 * DIL_HEADS * DIL_HD
MIX_A = MLA_HEADS * MLA_V + DIL_HEADS * DIL_HD
IN_C = (C_HEADS + 2 * C_KV_HEADS) * C_HD
MIX_C = C_HEADS * C_HD

kernel_name = 'hybrid_mla_dilated_swa_peer_encoder'


def layer_norm(x, g, b):
    xf = x.astype(jnp.float32)
    mu = xf.mean(-1, keepdims=True)
    var = jnp.square(xf - mu).mean(-1, keepdims=True)
    return ((xf - mu) * lax.rsqrt(var + LN_EPS) * g.astype(jnp.float32) + b.astype(jnp.float32)).astype(x.dtype)


def rms_norm(x, g):
    xf = x.astype(jnp.float32)
    return (xf * lax.rsqrt(jnp.square(xf).mean(-1, keepdims=True) + RMS_EPS) * g.astype(jnp.float32)).astype(x.dtype)


def rope(x, pos, rot_dim, theta):
    half = rot_dim // 2
    inv = theta ** (-jnp.arange(half, dtype=jnp.float32) / half)
    ang = pos.astype(jnp.float32)[:, None] * inv[None, :]
    cos, sin = jnp.cos(ang)[:, None, :], jnp.sin(ang)[:, None, :]
    x1 = x[..., :half].astype(jnp.float32)
    x2 = x[..., half:rot_dim].astype(jnp.float32)
    rot = jnp.concatenate([x1 * cos - x2 * sin, x2 * cos + x1 * sin], axis=-1).astype(x.dtype)
    return jnp.concatenate([rot, x[..., rot_dim:]], axis=-1)


def banded_attention(q, k, v, radius, blk, sink=None):
    n, L, H, dh = q.shape
    G = k.shape[2]
    rep = H // G
    nbr = -(-radius // blk)
    nb = -(-L // blk)
    lp = nb * blk
    qb = jnp.pad(q, ((0, 0), (0, lp - L), (0, 0), (0, 0))).reshape(n, nb, blk, G, rep, dh)
    padk = ((0, 0), (nbr * blk, lp - L + nbr * blk), (0, 0), (0, 0))
    kp = jnp.pad(k, padk)
    vp = jnp.pad(v, padk)
    win = lambda t: jnp.concatenate(
        [t[:, o * blk:o * blk + lp].reshape(n, nb, blk, G, dh) for o in range(2 * nbr + 1)], axis=2)
    kb, vb = win(kp), win(vp)
    s = jnp.einsum('nbqgrd,nbkgd->nbgrqk', qb, kb, preferred_element_type=jnp.float32) * (dh ** -0.5)
    qpos = jnp.arange(nb)[:, None, None] * blk + jnp.arange(blk)[None, :, None]
    kpos = (jnp.arange(nb)[:, None, None] - nbr) * blk + jnp.arange((2 * nbr + 1) * blk)[None, None, :]
    valid = (jnp.abs(kpos - qpos) <= radius) & (kpos >= 0) & (kpos < L)
    s = jnp.where(valid[None, :, None, None, :, :], s, NEG)
    m = s.max(-1)
    if sink is not None:
        sk = sink.astype(jnp.float32).reshape(G, rep)[None, None, :, :, None]
        m = jnp.maximum(m, sk)
    p = jnp.exp(s - m[..., None])
    den = p.sum(-1)
    if sink is not None:
        den = den + jnp.exp(sk - m)
    o = jnp.einsum('nbgrqk,nbkgd->nbqgrd', (p / den[..., None]).astype(v.dtype), vb,
                   preferred_element_type=jnp.float32).astype(q.dtype)
    o = o.reshape(n, lp, H, dh)[:, :L]
    lse = (m + jnp.log(den)).transpose(0, 1, 4, 2, 3).reshape(n, lp, H)[:, :L]
    return o, lse


def dilated_attention(q, k, v):
    n, S, H, dh = q.shape
    outs, lses = [], []
    for window, dil in DIL_PATTERNS:
        radius = (window // 2) // dil
        L = S // dil
        split = lambda t: t.reshape(n, L, dil, H, dh).transpose(0, 2, 1, 3, 4).reshape(n * dil, L, H, dh)
        o, lse = banded_attention(split(q), split(k), split(v), radius, DIL_BLOCK)
        outs.append(o.reshape(n, dil, L, H, dh).transpose(0, 2, 1, 3, 4).reshape(n, S, H, dh))
        lses.append(lse.reshape(n, dil, L, H).transpose(0, 2, 1, 3).reshape(n, S, H))
    w = jax.nn.softmax(jnp.stack(lses, axis=-1), axis=-1)
    return jnp.einsum('nshp,pnshd->nshd', w, jnp.stack(outs, axis=0).astype(jnp.float32)).astype(q.dtype)


def mla_attention(q_nope, q_rope, k_nope, k_rope, v):
    n, S, H, _ = q_nope.shape
    nq = S // MLA_Q_BLOCK
    scale = (MLA_NOPE + MLA_ROPE) ** -0.5
    blocks = lambda t: jnp.moveaxis(t.reshape(n, nq, MLA_Q_BLOCK, *t.shape[2:]), 1, 0)

    def one(qs):
        bn, br = qs
        s = (jnp.einsum('bqhd,bkhd->bhqk', bn, k_nope, preferred_element_type=jnp.float32)
             + jnp.einsum('bqhd,bkd->bhqk', br, k_rope, preferred_element_type=jnp.float32)) * scale
        p = jax.nn.softmax(s, axis=-1)
        return jnp.einsum('bhqk,bkhd->bqhd', p.astype(v.dtype), v, preferred_element_type=jnp.float32).astype(v.dtype)

    o = lax.map(one, (blocks(q_nope), blocks(q_rope)))
    return jnp.moveaxis(o, 0, 1).reshape(n, S, H, MLA_V)


def mixer_even(x, pos, w_in, q_norm, kv_norm, w_uq, w_ukv, w_out):
    n, S, _ = x.shape
    h = x @ w_in
    o1 = MLA_Q_LORA
    o2 = o1 + MLA_KV_LORA
    o3 = o2 + MLA_ROPE
    c_q, c_kv, k_r, qkv_b = h[..., :o1], h[..., o1:o2], h[..., o2:o3], h[..., o3:]
    q = (rms_norm(c_q, q_norm) @ w_uq).reshape(n, S, MLA_HEADS, MLA_NOPE + MLA_ROPE)
    q_nope = q[..., :MLA_NOPE]
    q_rope = rope(q[..., MLA_NOPE:], pos, MLA_ROPE, MLA_THETA)
    kv = (rms_norm(c_kv, kv_norm) @ w_ukv).reshape(n, S, MLA_HEADS, MLA_NOPE + MLA_V)
    k_nope, v_a = kv[..., :MLA_NOPE], kv[..., MLA_NOPE:]
    k_rope = rope(k_r[:, :, None, :], pos, MLA_ROPE, MLA_THETA)[:, :, 0]
    o_a = mla_attention(q_nope, q_rope, k_nope, k_rope, v_a)
    qkv = qkv_b.reshape(n, S, 3, DIL_HEADS, DIL_HD)
    rd = DIL_HD // ROPE_FRACTION
    qb = rope(qkv[:, :, 0], pos, rd, ROPE_THETA)
    kb = rope(qkv[:, :, 1], pos, rd, ROPE_THETA)
    o_b = dilated_attention(qb, kb, qkv[:, :, 2])
    o = jnp.concatenate([o_a.reshape(n, S, MLA_HEADS * MLA_V), o_b.reshape(n, S, DIL_HEADS * DIL_HD)], axis=-1)
    return o @ w_out


def mixer_odd(x, pos, w_in, sink, w_out):
    n, S, _ = x.shape
    h = x @ w_in
    e1 = C_HEADS * C_HD
    e2 = e1 + C_KV_HEADS * C_HD
    rd = C_HD // ROPE_FRACTION
    q = rope(h[..., :e1].reshape(n, S, C_HEADS, C_HD), pos, rd, ROPE_THETA)
    k = rope(h[..., e1:e2].reshape(n, S, C_KV_HEADS, C_HD), pos, rd, ROPE_THETA)
    v = h[..., e2:].reshape(n, S, C_KV_HEADS, C_HD)
    o, _ = banded_attention(q, k, v, C_RADIUS, C_BLOCK, sink)
    return o.reshape(n, S, MIX_C) @ w_out


def peer_ffn(x, w_q, sub_keys, u_tab, v_tab):
    n, S, D = x.shape
    T, K = PEER_TOKEN_BLOCK, PEER_TOPK
    xt = x.reshape(-1, T, D)

    def one(xb):
        q = (xb @ w_q).reshape(T, PEER_HEADS, 2, PEER_D_KEY // 2)
        s = jnp.einsum('thcd,hcnd->thcn', q, sub_keys, preferred_element_type=jnp.float32)
        sv, si = lax.top_k(s, K)
        cand = sv[:, :, 0, :, None] + sv[:, :, 1, None, :]
        cv, ci = lax.top_k(cand.reshape(T, PEER_HEADS, K * K), K)
        e = (jnp.take_along_axis(si[:, :, 0], ci // K, axis=-1) * PEER_N_KEYS
             + jnp.take_along_axis(si[:, :, 1], ci % K, axis=-1))
        g = jax.nn.softmax(cv, axis=-1)
        u = u_tab[e]
        vv = v_tab[e]
        a = jax.nn.gelu(jnp.einsum('td,thkd->thk', xb, u, preferred_element_type=jnp.float32), approximate=False)
        return jnp.einsum('thk,thkd->td', (g * a).astype(x.dtype), vv,
                          preferred_element_type=jnp.float32).astype(x.dtype)

    return lax.map(one, xt).reshape(n, S, D)


def trunk(x, p, a_w_in, a_q_norm, a_kv_norm, a_w_uq, a_w_ukv, a_w_out, c_w_in, c_sink, c_w_out,
          ln_mix_g, ln_mix_b, ln_ffn_g, ln_ffn_b, peer_w_q, peer_sub_keys, peer_u, peer_v, ple_proj, ple_gate):
    pos = jnp.arange(x.shape[1])
    for i in range(DEPTH):
        j = i // 2
        if i % 2 == 0:
            mix = mixer_even(x, pos, a_w_in[j], a_q_norm[j], a_kv_norm[j], a_w_uq[j], a_w_ukv[j], a_w_out[j])
        else:
            mix = mixer_odd(x, pos, c_w_in[j], c_sink[j], c_w_out[j])
        x = layer_norm(DN_ALPHA * x + mix, ln_mix_g[i], ln_mix_b[i])
        x = layer_norm(DN_ALPHA * x + peer_ffn(x, peer_w_q[i], peer_sub_keys[i], peer_u[i], peer_v[i]),
                       ln_ffn_g[i], ln_ffn_b[i])
        x = x + (p[i] @ ple_proj[i]) * jax.nn.sigmoid(x @ ple_gate[i])
    return x


def setup_inputs(seed: int = 0) -> dict:
    key = jax.random.key(seed)
    ks = jax.random.split(key, 24)
    nrm = lambda k, shape, scale: jax.random.normal(k, shape, jnp.float32) * scale
    gain = lambda k, shape: 1.0 + 0.02 * jax.random.normal(k, shape, jnp.float32)
    return {
        'x_prompt': nrm(ks[0], (BATCH, SEQ, D_MODEL), 1.0),
        'x_sample': nrm(ks[1], (DEC_BATCH, DEC_SEQ, D_MODEL), 1.0),
        'p_prompt': nrm(ks[2], (DEPTH, BATCH, SEQ, PLE_DIM), 1.0),
        'p_sample': nrm(ks[3], (DEPTH, DEC_BATCH, DEC_SEQ, PLE_DIM), 1.0),
        'a_w_in': nrm(ks[4], (N_EVEN, D_MODEL, IN_A), D_MODEL ** -0.5),
        'a_q_norm': gain(ks[5], (N_EVEN, MLA_Q_LORA)),
        'a_kv_norm': gain(ks[6], (N_EVEN, MLA_KV_LORA)),
        'a_w_uq': nrm(ks[7], (N_EVEN, MLA_Q_LORA, MLA_HEADS * (MLA_NOPE + MLA_ROPE)), MLA_Q_LORA ** -0.5),
        'a_w_ukv': nrm(ks[8], (N_EVEN, MLA_KV_LORA, MLA_HEADS * (MLA_NOPE + MLA_V)), MLA_KV_LORA ** -0.5),
        'a_w_out': nrm(ks[9], (N_EVEN, MIX_A, D_MODEL), DN_BETA * MIX_A ** -0.5),
        'c_w_in': nrm(ks[10], (N_ODD, D_MODEL, IN_C), D_MODEL ** -0.5),
        'c_sink': nrm(ks[11], (N_ODD, C_HEADS), 0.5),
        'c_w_out': nrm(ks[12], (N_ODD, MIX_C, D_MODEL), DN_BETA * MIX_C ** -0.5),
        'ln_mix_g': gain(ks[13], (DEPTH, D_MODEL)),
        'ln_mix_b': nrm(ks[14], (DEPTH, D_MODEL), 0.02),
        'ln_ffn_g': gain(ks[15], (DEPTH, D_MODEL)),
        'ln_ffn_b': nrm(ks[16], (DEPTH, D_MODEL), 0.02),
        'peer_w_q': nrm(ks[17], (DEPTH, D_MODEL, PEER_HEADS * PEER_D_KEY), D_MODEL ** -0.5),
        'peer_sub_keys': nrm(ks[18], (DEPTH, PEER_HEADS, 2, PEER_N_KEYS, PEER_D_KEY // 2), (PEER_D_KEY // 2) ** -0.5),
        'peer_u': nrm(ks[19], (DEPTH, PEER_N_EXPERTS, D_MODEL), D_MODEL ** -0.5),
        'peer_v': nrm(ks[20], (DEPTH, PEER_N_EXPERTS, D_MODEL), DN_BETA),
        'ple_proj': nrm(ks[21], (DEPTH, PLE_DIM, D_MODEL), PLE_DIM ** -0.5),
        'ple_gate': nrm(ks[22], (DEPTH, D_MODEL, D_MODEL), D_MODEL ** -0.5),
    }


def reference(x_prompt, x_sample, p_prompt, p_sample, a_w_in, a_q_norm, a_kv_norm, a_w_uq, a_w_ukv, a_w_out,
              c_w_in, c_sink, c_w_out, ln_mix_g, ln_mix_b, ln_ffn_g, ln_ffn_b, peer_w_q, peer_sub_keys,
              peer_u, peer_v, ple_proj, ple_gate):
    y_prompt = trunk(x_prompt, p_prompt, a_w_in, a_q_norm, a_kv_norm, a_w_uq, a_w_ukv, a_w_out, c_w_in, c_sink,
                     c_w_out, ln_mix_g, ln_mix_b, ln_ffn_g, ln_ffn_b, peer_w_q, peer_sub_keys, peer_u, peer_v,
                     ple_proj, ple_gate)
    y_sample = trunk(x_sample, p_sample, a_w_in, a_q_norm, a_kv_norm, a_w_uq, a_w_ukv, a_w_out, c_w_in, c_sink,
                     c_w_out, ln_mix_g, ln_mix_b, ln_ffn_g, ln_ffn_b, peer_w_q, peer_sub_keys, peer_u, peer_v,
                     ple_proj, ple_gate)
    return (y_prompt, y_sample)
```

```python
import functools
import math

import jax
import jax.numpy as jnp
import numpy as np
from jax import lax
from jax.experimental import pallas as pl
from jax.experimental.pallas import tpu as pltpu

F32 = jnp.float32
BF16 = jnp.bfloat16

D_MODEL = 1024
DEPTH = 2
PLE_DIM = 256
MLA_HEADS = 8
MLA_Q_LORA = 384
MLA_KV_LORA = 256
MLA_NOPE = 64
MLA_ROPE = 32
MLA_V = 64
MLA_THETA = 10000.0
MLA_SLOT = 128
DIL_HEADS = 8
DIL_HD = 64
DIL_PATTERNS = ((128, 1), (512, 4), (2048, 16))
C_HEADS = 16
C_KV_HEADS = 4
C_HD = 64
C_RADIUS = 128
ROPE_THETA = 500000.0
ROPE_FRACTION = 4
PEER_HEADS = 8
PEER_N_KEYS = 128
PEER_N_EXPERTS = PEER_N_KEYS * PEER_N_KEYS
PEER_D_KEY = 128
PEER_TOPK = 16
DN_ALPHA = (2 * DEPTH) ** 0.25
LN_EPS = 1e-5
RMS_EPS = 1e-6
NEG = -1e30
INV_SQRT2 = 0.7071067811865476

LANES = 128
SUBLANES = 8
VMEM_LIMIT_BYTES = 56 * 1024 * 1024

TM = 512
TQ_MLA = 512
TK_MLA = 512
T_DIL = 256
DIL_REACH = 1024
T_WIN = 256
TM_SEL = 256
TM_PEER = 512
EB_PEER = 1024

_NT = (((1,), (1,)), ((), ()))


def _params(*sem):
    return pltpu.CompilerParams(dimension_semantics=sem, vmem_limit_bytes=VMEM_LIMIT_BYTES)


def _layer_norm(y, g, b):
    mu = jnp.mean(y, axis=-1, keepdims=True)
    d = y - mu
    var = jnp.mean(d * d, axis=-1, keepdims=True)
    return d * lax.rsqrt(var + LN_EPS) * g + b


def _rms_norm(y, g):
    return y * lax.rsqrt(jnp.mean(y * y, axis=-1, keepdims=True) + RMS_EPS) * g


def _tile_lanes(t, reps):
    return jnp.concatenate([t] * reps, axis=-1)


def _inproj_even_kernel(x_ref, wcq_ref, wckv_ref, wkr_ref, wqb_ref, wkb_ref, wvb_ref,
                        qn_ref, kvn_ref, wuq_ref, wk_ref, wv_ref, ta_ref, tb_ref,
                        qa_ref, ka_ref, va_ref, qb_ref, kb_ref, vb_ref):
    xb = x_ref[...].astype(BF16)
    dot = functools.partial(jnp.dot, preferred_element_type=F32)
    cq = _rms_norm(dot(xb, wcq_ref[...]), qn_ref[...]).astype(BF16)
    q12 = dot(cq, wuq_ref[...])
    cq_t = _tile_lanes(ta_ref[0], MLA_HEADS)
    sq_t = _tile_lanes(ta_ref[1], MLA_HEADS)
    qa_ref[...] = (q12[:, :1024] * cq_t + q12[:, 1024:] * sq_t).astype(BF16)
    ckv = _rms_norm(dot(xb, wckv_ref[...]), kvn_ref[...]).astype(BF16)
    kr12 = dot(xb, wkr_ref[...])
    kr = kr12[:, :LANES] * ta_ref[2] + kr12[:, LANES:] * ta_ref[3]
    ka_ref[...] = (dot(ckv, wk_ref[...]) + _tile_lanes(kr, MLA_HEADS)).astype(BF16)
    va_ref[...] = dot(ckv, wv_ref[...]).astype(BF16)
    q12b = dot(xb, wqb_ref[...])
    qb_ref[...] = (q12b[:, :512] * _tile_lanes(tb_ref[0], 4)
                   + q12b[:, 512:] * _tile_lanes(tb_ref[1], 4)).astype(BF16)
    k12b = dot(xb, wkb_ref[...])
    kb_ref[...] = (k12b[:, :512] * _tile_lanes(tb_ref[2], 4)
                   + k12b[:, 512:] * _tile_lanes(tb_ref[3], 4)).astype(BF16)
    vb_ref[...] = dot(xb, wvb_ref[...]).astype(BF16)


def _const_spec(shape):
    nd = len(shape)
    return pl.BlockSpec(shape, lambda *_: (0,) * nd)


def _inproj_even(x, w, ta, tb, seq):
    n = x.shape[0]
    pos_blocks = seq // TM
    row = lambda i: (i, 0)
    tab = lambda i: (0, i % pos_blocks, 0)
    wnames = ("wcq", "wckv", "wkr", "wqb", "wkb", "wvb", "qn", "kvn", "wuq", "wk", "wv")
    ws = [w[k] for k in wnames]
    out_w = (1024, 1024, 512, 512, 512, 512)
    return pl.pallas_call(
        _inproj_even_kernel,
        grid=(n // TM,),
        in_specs=[pl.BlockSpec((TM, D_MODEL), row)] + [_const_spec(a.shape) for a in ws]
        + [pl.BlockSpec((4, TM, LANES), tab), pl.BlockSpec((4, TM, LANES), tab)],
        out_specs=[pl.BlockSpec((TM, c), row) for c in out_w],
        out_shape=[jax.ShapeDtypeStruct((n, c), BF16) for c in out_w],
        compiler_params=_params("parallel"),
        name="inproj_even",
    )(x, *ws, ta, tb)


def _mla_kernel(q_ref, k_ref, v_ref, o_ref, m_ref, l_ref, acc_ref):
    ki = pl.program_id(2)

    @pl.when(ki == 0)
    def _init():
        m_ref[...] = jnp.full(m_ref.shape, NEG, F32)
        l_ref[...] = jnp.zeros(l_ref.shape, F32)
        acc_ref[...] = jnp.zeros(acc_ref.shape, F32)

    for h in range(MLA_HEADS):
        qh = q_ref[:, h * MLA_SLOT:(h + 1) * MLA_SLOT]
        kh = k_ref[:, h * MLA_SLOT:(h + 1) * MLA_SLOT]
        s = lax.dot_general(qh, kh, _NT, preferred_element_type=F32)
        m_prev = m_ref[h]
        m_new = jnp.maximum(m_prev, jnp.max(s, axis=-1, keepdims=True))
        alpha = jnp.exp(m_prev - m_new)
        p = jnp.exp(s - m_new)
        l_ref[h] = alpha * l_ref[h] + jnp.sum(p, axis=-1, keepdims=True)
        vh = v_ref[:, h * MLA_V:(h + 1) * MLA_V]
        pv = jnp.dot(p.astype(BF16), vh, preferred_element_type=F32)
        acc_ref[h] = alpha * acc_ref[h] + pv
        m_ref[h] = m_new

    @pl.when(ki == pl.num_programs(2) - 1)
    def _fin():
        outs = [acc_ref[h] / l_ref[h] for h in range(MLA_HEADS)]
        o_ref[...] = jnp.concatenate(outs, axis=-1).astype(BF16)


def _mla_attention(q, k, v, n_seq, seq):
    n = q.shape[0]
    nq, nk = seq // TQ_MLA, seq // TK_MLA
    return pl.pallas_call(
        _mla_kernel,
        grid=(n_seq, nq, nk),
        in_specs=[pl.BlockSpec((TQ_MLA, 1024), lambda b, i, j: (b * nq + i, 0)),
                  pl.BlockSpec((TK_MLA, 1024), lambda b, i, j: (b * nk + j, 0)),
                  pl.BlockSpec((TK_MLA, 512), lambda b, i, j: (b * nk + j, 0))],
        out_specs=pl.BlockSpec((TQ_MLA, 512), lambda b, i, j: (b * nq + i, 0)),
        out_shape=jax.ShapeDtypeStruct((n, 512), BF16),
        scratch_shapes=[pltpu.VMEM((MLA_HEADS, TQ_MLA, 1), F32),
                        pltpu.VMEM((MLA_HEADS, TQ_MLA, 1), F32),
                        pltpu.VMEM((MLA_HEADS, TQ_MLA, MLA_V), F32)],
        compiler_params=_params("parallel", "parallel", "arbitrary"),
        name="mla_attention",
    )(q, k, v)


def _dilated_log_count():
    nblk = 2 * (DIL_REACH // T_DIL) + 1
    row = np.arange(T_DIL)[:, None]
    col = np.arange(T_DIL)[None, :]
    out = np.empty((nblk, T_DIL, T_DIL), np.float32)
    for kj in range(nblk):
        d = (kj - DIL_REACH // T_DIL) * T_DIL + col - row
        c = np.zeros_like(d)
        for window, dil in DIL_PATTERNS:
            radius = (window // 2) // dil
            c = c + ((d % dil == 0) & (np.abs(d) <= radius * dil))
        out[kj] = np.where(c > 0, np.log(np.maximum(c, 1)), NEG)
    return out


def _dilated_kernel(q_ref, k_ref, v_ref, lc_ref, o_ref, m_ref, l_ref, acc_ref, *, nblocks):
    qi = pl.program_id(1)
    kj = pl.program_id(2)
    half = DIL_REACH // T_DIL
    kblk = qi + kj - half

    @pl.when(kj == 0)
    def _init():
        m_ref[...] = jnp.full(m_ref.shape, NEG, F32)
        l_ref[...] = jnp.zeros(l_ref.shape, F32)
        acc_ref[...] = jnp.zeros(acc_ref.shape, F32)

    @pl.when((kblk >= 0) & (kblk < nblocks))
    def _step():
        lc = lc_ref[0]
        for h in range(DIL_HEADS):
            qh = q_ref[:, h * DIL_HD:(h + 1) * DIL_HD]
            kh = k_ref[:, h * DIL_HD:(h + 1) * DIL_HD]
            s = lax.dot_general(qh, kh, _NT, preferred_element_type=F32) + lc
            m_prev = m_ref[h]
            m_new = jnp.maximum(m_prev, jnp.max(s, axis=-1, keepdims=True))
            alpha = jnp.exp(m_prev - m_new)
            p = jnp.exp(s - m_new)
            l_ref[h] = alpha * l_ref[h] + jnp.sum(p, axis=-1, keepdims=True)
            vh = v_ref[:, h * DIL_HD:(h + 1) * DIL_HD]
            acc_ref[h] = alpha * acc_ref[h] + jnp.dot(p.astype(BF16), vh, preferred_element_type=F32)
            m_ref[h] = m_new

    @pl.when(kj == pl.num_programs(2) - 1)
    def _fin():
        outs = [acc_ref[h] / l_ref[h] for h in range(DIL_HEADS)]
        o_ref[...] = jnp.concatenate(outs, axis=-1).astype(BF16)


def _dilated_attention(q, k, v, n_seq, seq):
    n = q.shape[0]
    nb = seq // T_DIL
    half = DIL_REACH // T_DIL
    nwin = 2 * half + 1
    lc = jnp.asarray(_dilated_log_count())
    kv_idx = lambda b, i, j: (b * nb + jnp.clip(i + j - half, 0, nb - 1), 0)
    return pl.pallas_call(
        functools.partial(_dilated_kernel, nblocks=nb),
        grid=(n_seq, nb, nwin),
        in_specs=[pl.BlockSpec((T_DIL, 512), lambda b, i, j: (b * nb + i, 0)),
                  pl.BlockSpec((T_DIL, 512), kv_idx),
                  pl.BlockSpec((T_DIL, 512), kv_idx),
                  pl.BlockSpec((1, T_DIL, T_DIL), lambda b, i, j: (j, 0, 0))],
        out_specs=pl.BlockSpec((T_DIL, 512), lambda b, i, j: (b * nb + i, 0)),
        out_shape=jax.ShapeDtypeStruct((n, 512), BF16),
        scratch_shapes=[pltpu.VMEM((DIL_HEADS, T_DIL, 1), F32),
                        pltpu.VMEM((DIL_HEADS, T_DIL, 1), F32),
                        pltpu.VMEM((DIL_HEADS, T_DIL, DIL_HD), F32)],
        compiler_params=_params("parallel", "parallel", "arbitrary"),
        name="dilated_attention",
    )(q, k, v, lc)


def _outproj_kernel(*refs, n_in):
    o_refs = refs[:n_in]
    w_refs = refs[n_in:2 * n_in]
    x_ref, g_ref, b_ref, y_ref = refs[2 * n_in:]
    mix = jnp.dot(o_refs[0][...], w_refs[0][...], preferred_element_type=F32)
    for o_r, w_r in zip(o_refs[1:], w_refs[1:]):
        mix = mix + jnp.dot(o_r[...], w_r[...], preferred_element_type=F32)
    y_ref[...] = _layer_norm(DN_ALPHA * x_ref[...] + mix, g_ref[...], b_ref[...])


def _outproj_ln(os_, ws, x, g, b):
    n = x.shape[0]
    row = lambda i: (i, 0)
    return pl.pallas_call(
        functools.partial(_outproj_kernel, n_in=len(os_)),
        grid=(n // TM,),
        in_specs=[pl.BlockSpec((TM, o.shape[1]), row) for o in os_]
        + [_const_spec(w.shape) for w in ws]
        + [pl.BlockSpec((TM, D_MODEL), row), _const_spec(g.shape), _const_spec(b.shape)],
        out_specs=pl.BlockSpec((TM, D_MODEL), row),
        out_shape=jax.ShapeDtypeStruct((n, D_MODEL), F32),
        compiler_params=_params("parallel"),
        name="outproj_ln",
    )(*os_, *ws, x, g, b)


def _inproj_odd_kernel(x_ref, wq_ref, wk_ref, wv_ref, tb_ref, q_ref, k_ref, v_ref):
    xb = x_ref[...].astype(BF16)
    dot = functools.partial(jnp.dot, preferred_element_type=F32)
    q12 = dot(xb, wq_ref[...])
    q_ref[...] = (q12[:, :1024] * _tile_lanes(tb_ref[0], 8)
                  + q12[:, 1024:] * _tile_lanes(tb_ref[1], 8)).astype(BF16)
    k12 = dot(xb, wk_ref[...])
    k_ref[...] = (k12[:, :256] * _tile_lanes(tb_ref[2], 2)
                  + k12[:, 256:] * _tile_lanes(tb_ref[3], 2)).astype(BF16)
    v_ref[...] = dot(xb, wv_ref[...]).astype(BF16)


def _inproj_odd(x, w, tb, seq):
    n = x.shape[0]
    pos_blocks = seq // TM
    row = lambda i: (i, 0)
    ws = [w["wq"], w["wk"], w["wv"]]
    out_w = (1024, 256, 256)
    return pl.pallas_call(
        _inproj_odd_kernel,
        grid=(n // TM,),
        in_specs=[pl.BlockSpec((TM, D_MODEL), row)] + [_const_spec(a.shape) for a in ws]
        + [pl.BlockSpec((4, TM, LANES), lambda i: (0, i % pos_blocks, 0))],
        out_specs=[pl.BlockSpec((TM, c), row) for c in out_w],
        out_shape=[jax.ShapeDtypeStruct((n, c), BF16) for c in out_w],
        compiler_params=_params("parallel"),
        name="inproj_odd",
    )(x, *ws, tb)


def _window_bias():
    row = np.arange(T_WIN)[:, None]
    col = np.arange(3 * T_WIN)[None, :]
    d = col - T_WIN - row
    return np.where(np.abs(d) <= C_RADIUS, 0.0, NEG).astype(np.float32)


def _windowed_kernel(sink_ref, q_ref, kp_ref, kc_ref, kn_ref, vp_ref, vc_ref, vn_ref, bias_ref, o_ref):
    qi = pl.program_id(1)
    nb = pl.num_programs(1)
    bias = bias_ref[...]
    edge = jnp.where(qi > 0, 0.0, NEG)
    edge_n = jnp.where(qi < nb - 1, 0.0, NEG)
    col = lax.broadcasted_iota(jnp.int32, (1, 3 * T_WIN), 1)
    bias = bias + jnp.where(col < T_WIN, edge, 0.0) + jnp.where(col >= 2 * T_WIN, edge_n, 0.0)
    rep = C_HEADS // C_KV_HEADS
    outs = []
    for g in range(C_KV_HEADS):
        sl = slice(g * C_HD, (g + 1) * C_HD)
        kcat = jnp.concatenate([kp_ref[:, sl], kc_ref[:, sl], kn_ref[:, sl]], axis=0)
        vcat = jnp.concatenate([vp_ref[:, sl], vc_ref[:, sl], vn_ref[:, sl]], axis=0)
        for r in range(rep):
            h = g * rep + r
            qh = q_ref[:, h * C_HD:(h + 1) * C_HD]
            s = lax.dot_general(qh, kcat, _NT, preferred_element_type=F32) + bias
            sk = sink_ref[h]
            m = jnp.maximum(jnp.max(s, axis=-1, keepdims=True), sk)
            p = jnp.exp(s - m)
            den = jnp.sum(p, axis=-1, keepdims=True) + jnp.exp(sk - m)
            outs.append(jnp.dot(p.astype(BF16), vcat, preferred_element_type=F32) / den)
    o_ref[...] = jnp.concatenate(outs, axis=-1).astype(BF16)


def _windowed_attention(q, k, v, sink, n_seq, seq):
    n = q.shape[0]
    nb = seq // T_WIN
    cur = lambda b, i: (b * nb + i, 0)
    prev = lambda b, i: (b * nb + jnp.maximum(i - 1, 0), 0)
    nxt = lambda b, i: (b * nb + jnp.minimum(i + 1, nb - 1), 0)
    kvw = C_KV_HEADS * C_HD
    return pl.pallas_call(
        _windowed_kernel,
        grid=(n_seq, nb),
        in_specs=[pl.BlockSpec(memory_space=pltpu.SMEM),
                  pl.BlockSpec((T_WIN, 1024), cur),
                  pl.BlockSpec((T_WIN, kvw), prev), pl.BlockSpec((T_WIN, kvw), cur),
                  pl.BlockSpec((T_WIN, kvw), nxt),
                  pl.BlockSpec((T_WIN, kvw), prev), pl.BlockSpec((T_WIN, kvw), cur),
                  pl.BlockSpec((T_WIN, kvw), nxt),
                  _const_spec((T_WIN, 3 * T_WIN))],
        out_specs=pl.BlockSpec((T_WIN, 1024), cur),
        out_shape=jax.ShapeDtypeStruct((n, 1024), BF16),
        compiler_params=_params("parallel", "parallel"),
        name="windowed_attention",
    )(sink, q, k, k, k, v, v, v, jnp.asarray(_window_bias()))


def _extract_top(cur, k, out_ref=None, lanes=None):
    vals = []
    for i in range(k):
        mx = jnp.max(cur, axis=0, keepdims=True)
        vals.append(mx)
        if out_ref is not None:
            out_ref[i:i + 1, lanes] = mx
        if i + 1 < k:
            cur = jnp.where(cur >= mx, NEG, cur)
    return vals


def _peer_select_kernel(x_ref, wq_ref, keys_ref, s2_ref, e2_ref, thr_ref, e1_ref, q_ref, a_ref, b_ref):
    h = pl.program_id(1)

    @pl.when(h == 0)
    def _project():
        q_ref[...] = lax.dot_general(wq_ref[...], x_ref[...], _NT, preferred_element_type=F32,
                                     precision=lax.Precision.HIGHEST)

    half = PEER_D_KEY // 2
    row0 = pl.multiple_of(h * PEER_D_KEY, PEER_D_KEY)
    s1 = jnp.dot(keys_ref[0, 0], q_ref[pl.ds(row0, half), :], preferred_element_type=F32,
                 precision=lax.Precision.HIGHEST)
    s2 = jnp.dot(keys_ref[0, 1], q_ref[pl.ds(row0 + half, half), :], preferred_element_type=F32,
                 precision=lax.Precision.HIGHEST)
    for lt in range(TM_SEL // LANES):
        ls = slice(lt * LANES, (lt + 1) * LANES)
        s1t, s2t = s1[:, ls], s2[:, ls]
        _extract_top(s1t, PEER_TOPK + 1, a_ref, ls)
        _extract_top(s2t, PEER_TOPK + 1, b_ref, ls)
        a16, b16 = a_ref[0:PEER_TOPK, ls], b_ref[0:PEER_TOPK, ls]
        a0, b0 = a16[0:1], b16[0:1]
        cand = [a0 + b16]
        cand += [a16[i:i + 1] + b16[0:8] for i in range(1, 8)]
        cand += [a16[8:16] + b0]
        cand = jnp.concatenate(cand, axis=0)
        tops = _extract_top(cand, PEER_TOPK + 1)
        v16 = tops[PEER_TOPK - 1]
        v17 = jnp.maximum(tops[PEER_TOPK], jnp.maximum(a0 + b_ref[PEER_TOPK:PEER_TOPK + 1, ls],
                                                      a_ref[PEER_TOPK:PEER_TOPK + 1, ls] + b0))
        tau = 0.5 * (v16 + v17)
        ez = jnp.where(cand >= v16, jnp.exp(cand - (a0 + b0)), 0.0)
        z = jnp.sum(ez, axis=0, keepdims=True)
        s2_ref[0, :, ls] = s2t
        e2_ref[0, :, ls] = jnp.exp(s2t - b0) * (0.5 / z)
        thr_ref[0, :, ls] = tau - s1t
        e1_ref[0, :, ls] = jnp.exp(s1t - a0)


def _peer_select(x, wq_t, keys):
    n = x.shape[0]
    stat = jax.ShapeDtypeStruct((PEER_HEADS, PEER_N_KEYS, n), F32)
    stat_spec = pl.BlockSpec((1, PEER_N_KEYS, TM_SEL), lambda i, h: (h, 0, i))
    return pl.pallas_call(
        _peer_select_kernel,
        grid=(n // TM_SEL, PEER_HEADS),
        in_specs=[pl.BlockSpec((TM_SEL, D_MODEL), lambda i, h: (i, 0)),
                  _const_spec(wq_t.shape),
                  pl.BlockSpec((1, 2, PEER_N_KEYS, PEER_D_KEY // 2), lambda i, h: (h, 0, 0, 0))],
        out_specs=[stat_spec] * 4,
        out_shape=[stat] * 4,
        scratch_shapes=[pltpu.VMEM((PEER_HEADS * PEER_D_KEY, TM_SEL), F32),
                        pltpu.VMEM((PEER_TOPK + SUBLANES, TM_SEL), F32),
                        pltpu.VMEM((PEER_TOPK + SUBLANES, TM_SEL), F32)],
        compiler_params=_params("parallel", "arbitrary"),
        name="peer_select",
    )(x, wq_t, keys)


def _peer_expert_kernel(x_ref, s2_ref, e2_ref, thr_ref, e1_ref, u_ref, vt_ref, g_ref, b_ref, y_ref,
                        xb_ref, a_ref, h_ref, acc_ref):
    eb = pl.program_id(1)
    n_i1 = EB_PEER // PEER_N_KEYS

    @pl.when(eb == 0)
    def _init():
        xb_ref[...] = x_ref[...].astype(BF16)
        acc_ref[...] = jnp.zeros(acc_ref.shape, F32)

    a_ref[...] = lax.dot_general(u_ref[...], xb_ref[...], _NT, preferred_element_type=F32)
    i1_base = pl.multiple_of(eb * n_i1, SUBLANES)
    for lt in range(TM_PEER // LANES):
        ls = slice(lt * LANES, (lt + 1) * LANES)
        thr8 = [thr_ref[h, pl.ds(i1_base, n_i1), ls] for h in range(PEER_HEADS)]
        e18 = [e1_ref[h, pl.ds(i1_base, n_i1), ls] for h in range(PEER_HEADS)]
        for r in range(n_i1):
            gate = jnp.zeros((PEER_N_KEYS, LANES), F32)
            for h in range(PEER_HEADS):
                keep = s2_ref[h, :, ls] >= thr8[h][r:r + 1]
                gate = gate + jnp.where(keep, e2_ref[h, :, ls], 0.0) * e18[h][r:r + 1]
            a = a_ref[r * PEER_N_KEYS:(r + 1) * PEER_N_KEYS, ls]
            hv = a * (1.0 + lax.erf(a * INV_SQRT2)) * gate
            h_ref[r * PEER_N_KEYS:(r + 1) * PEER_N_KEYS, ls] = hv.astype(BF16)
    acc_ref[...] += jnp.dot(vt_ref[...], h_ref[...], preferred_element_type=F32)

    @pl.when(eb == pl.num_programs(1) - 1)
    def _fin():
        y = DN_ALPHA * x_ref[...] + acc_ref[...].T
        y_ref[...] = _layer_norm(y, g_ref[...], b_ref[...])


def _peer_experts(x, stats, u_bf, vt_bf, g, b):
    n = x.shape[0]
    stat_spec = pl.BlockSpec((PEER_HEADS, PEER_N_KEYS, TM_PEER), lambda i, e: (0, 0, i))
    return pl.pallas_call(
        _peer_expert_kernel,
        grid=(n // TM_PEER, PEER_N_EXPERTS // EB_PEER),
        in_specs=[pl.BlockSpec((TM_PEER, D_MODEL), lambda i, e: (i, 0))] + [stat_spec] * 4
        + [pl.BlockSpec((EB_PEER, D_MODEL), lambda i, e: (e, 0)),
           pl.BlockSpec((D_MODEL, EB_PEER), lambda i, e: (0, e)),
           _const_spec(g.shape), _const_spec(b.shape)],
        out_specs=pl.BlockSpec((TM_PEER, D_MODEL), lambda i, e: (i, 0)),
        out_shape=jax.ShapeDtypeStruct((n, D_MODEL), F32),
        scratch_shapes=[pltpu.VMEM((TM_PEER, D_MODEL), BF16),
                        pltpu.VMEM((EB_PEER, TM_PEER), F32),
                        pltpu.VMEM((EB_PEER, TM_PEER), BF16),
                        pltpu.VMEM((D_MODEL, TM_PEER), F32)],
        compiler_params=_params("parallel", "arbitrary"),
        name="peer_experts",
    )(x, *stats, u_bf, vt_bf, g, b)


def _ple_kernel(x_ref, p_ref, wp_ref, wg_ref, y_ref):
    x = x_ref[...]
    proj = jnp.dot(p_ref[...].astype(BF16), wp_ref[...], preferred_element_type=F32)
    gate = jax.nn.sigmoid(jnp.dot(x.astype(BF16), wg_ref[...], preferred_element_type=F32))
    y_ref[...] = x + proj * gate


def _ple(x, p, wp, wg):
    n = x.shape[0]
    row = lambda i: (i, 0)
    return pl.pallas_call(
        _ple_kernel,
        grid=(n // TM,),
        in_specs=[pl.BlockSpec((TM, D_MODEL), row), pl.BlockSpec((TM, PLE_DIM), row),
                  _const_spec(wp.shape), _const_spec(wg.shape)],
        out_specs=pl.BlockSpec((TM, D_MODEL), row),
        out_shape=jax.ShapeDtypeStruct((n, D_MODEL), F32),
        compiler_params=_params("parallel"),
        name="ple_gate",
    )(x, p, wp, wg)


def _rope_partner_cols(w, head_dim, half):
    n_heads = w.shape[1] // head_dim
    w3 = w.reshape(w.shape[0], n_heads, head_dim)
    sw = jnp.concatenate([w3[..., half:2 * half], w3[..., :half],
                          jnp.zeros_like(w3[..., 2 * half:])], axis=-1)
    return sw.reshape(w.shape)


def _rope_tables(seq, half, theta, lanes_before, lanes_after, pass_through, scale):
    inv = theta ** (-jnp.arange(half, dtype=F32) / half)
    ang = jnp.arange(seq).astype(F32)[:, None] * inv[None, :]
    cos, sin = jnp.cos(ang), jnp.sin(ang)
    ones = lambda w, v: jnp.full((seq, w), v, F32)
    ct = jnp.concatenate([ones(lanes_before, pass_through), cos, cos, ones(lanes_after, pass_through)], axis=-1)
    st = jnp.concatenate([ones(lanes_before, 0.0), -sin, sin, ones(lanes_after, 0.0)], axis=-1)
    return ct * scale, st * scale


def _prep_even(a_w_in, a_q_norm, a_kv_norm, a_w_uq, a_w_ukv, a_w_out):
    o1 = MLA_Q_LORA
    o2 = o1 + MLA_KV_LORA
    o3 = o2 + MLA_ROPE
    nb = DIL_HEADS * DIL_HD
    bf = lambda t: t.astype(BF16)
    w = {}
    w["wcq"] = bf(a_w_in[:, :o1])
    w["wckv"] = bf(a_w_in[:, o1:o2])
    kr = a_w_in[:, o2:o3]
    krs = jnp.concatenate([kr[:, MLA_ROPE // 2:], kr[:, :MLA_ROPE // 2]], axis=-1)
    pad_slot = lambda t: jnp.pad(t, ((0, 0), (MLA_NOPE, MLA_SLOT - MLA_NOPE - MLA_ROPE)))
    w["wkr"] = bf(jnp.concatenate([pad_slot(kr), pad_slot(krs)], axis=-1))
    qb = a_w_in[:, o3:o3 + nb]
    kb = a_w_in[:, o3 + nb:o3 + 2 * nb]
    rd2 = DIL_HD // ROPE_FRACTION // 2
    w["wqb"] = bf(jnp.concatenate([qb, _rope_partner_cols(qb, DIL_HD, rd2)], axis=-1))
    w["wkb"] = bf(jnp.concatenate([kb, _rope_partner_cols(kb, DIL_HD, rd2)], axis=-1))
    w["wvb"] = bf(a_w_in[:, o3 + 2 * nb:])
    w["qn"] = a_q_norm.reshape(1, -1)
    w["kvn"] = a_kv_norm.reshape(1, -1)
    uq = a_w_uq.reshape(MLA_Q_LORA, MLA_HEADS, MLA_NOPE + MLA_ROPE)
    zpad = jnp.zeros((MLA_Q_LORA, MLA_HEADS, MLA_SLOT - MLA_NOPE - MLA_ROPE), F32)
    uq1 = jnp.concatenate([uq, zpad], axis=-1).reshape(MLA_Q_LORA, MLA_HEADS * MLA_SLOT)
    r = uq[..., MLA_NOPE:]
    uq2 = jnp.concatenate([jnp.zeros_like(uq[..., :MLA_NOPE]), r[..., MLA_ROPE // 2:], r[..., :MLA_ROPE // 2],
                           zpad], axis=-1).reshape(MLA_Q_LORA, MLA_HEADS * MLA_SLOT)
    w["wuq"] = bf(jnp.concatenate([uq1, uq2], axis=-1))
    ukv = a_w_ukv.reshape(MLA_KV_LORA, MLA_HEADS, MLA_NOPE + MLA_V)
    kpad = jnp.zeros((MLA_KV_LORA, MLA_HEADS, MLA_SLOT - MLA_NOPE), F32)
    w["wk"] = bf(jnp.concatenate([ukv[..., :MLA_NOPE], kpad], axis=-1).reshape(MLA_KV_LORA, -1))
    w["wv"] = bf(ukv[..., MLA_NOPE:].reshape(MLA_KV_LORA, MLA_HEADS * MLA_V))
    w["wout_a"] = bf(a_w_out[:MLA_HEADS * MLA_V])
    w["wout_b"] = bf(a_w_out[MLA_HEADS * MLA_V:])
    return w


def _prep_odd(c_w_in, c_w_out):
    e1 = C_HEADS * C_HD
    e2 = e1 + C_KV_HEADS * C_HD
    rd2 = C_HD // ROPE_FRACTION // 2
    q, k = c_w_in[:, :e1], c_w_in[:, e1:e2]
    return {
        "wq": jnp.concatenate([q, _rope_partner_cols(q, C_HD, rd2)], axis=-1).astype(BF16),
        "wk": jnp.concatenate([k, _rope_partner_cols(k, C_HD, rd2)], axis=-1).astype(BF16),
        "wv": c_w_in[:, e2:].astype(BF16),
        "wout": c_w_out.astype(BF16),
    }


def _tables_mla(seq):
    scale = (MLA_NOPE + MLA_ROPE) ** -0.5
    pad = MLA_SLOT - MLA_NOPE - MLA_ROPE
    cq, sq = _rope_tables(seq, MLA_ROPE // 2, MLA_THETA, MLA_NOPE, pad, 1.0, scale)
    ck, sk = _rope_tables(seq, MLA_ROPE // 2, MLA_THETA, MLA_NOPE, pad, 0.0, 1.0)
    return jnp.stack([cq, sq, ck, sk])


def _tables_partial(seq, head_dim, q_scale):
    half = head_dim // ROPE_FRACTION // 2
    cq, sq = _rope_tables(seq, half, ROPE_THETA, 0, head_dim - 2 * half, 1.0, q_scale)
    ck, sk = _rope_tables(seq, half, ROPE_THETA, 0, head_dim - 2 * half, 1.0, 1.0)
    two = lambda t: jnp.concatenate([t, t], axis=-1)
    return jnp.stack([two(cq), two(sq), two(ck), two(sk)])


def _peer_layer(x, wq_t, keys, u_bf, vt_bf, g, b):
    stats = _peer_select(x, wq_t, keys)
    return _peer_experts(x, stats, u_bf, vt_bf, g, b)


def _trunk(x3, p4, prm):
    n_seq, seq, _ = x3.shape
    x = x3.reshape(n_seq * seq, D_MODEL)
    p = p4.reshape(DEPTH, n_seq * seq, PLE_DIM)
    row = lambda t: t.reshape(1, -1)
    for i in range(DEPTH):
        if i % 2 == 0:
            w = prm["even"]
            qa, ka, va, qb, kb, vb = _inproj_even(x, w, _tables_mla(seq),
                                                  _tables_partial(seq, DIL_HD, DIL_HD ** -0.5), seq)
            oa = _mla_attention(qa, ka, va, n_seq, seq)
            ob = _dilated_attention(qb, kb, vb, n_seq, seq)
            x = _outproj_ln([oa, ob], [w["wout_a"], w["wout_b"]], x,
                            row(prm["ln_mix_g"][i]), row(prm["ln_mix_b"][i]))
        else:
            w = prm["odd"]
            q, k, v = _inproj_odd(x, w, _tables_partial(seq, C_HD, C_HD ** -0.5), seq)
            o = _windowed_attention(q, k, v, prm["sink"], n_seq, seq)
            x = _outproj_ln([o], [w["wout"]], x, row(prm["ln_mix_g"][i]), row(prm["ln_mix_b"][i]))
        x = _peer_layer(x, prm["peer_wq_t"][i], prm["peer_keys"][i], prm["peer_u"][i], prm["peer_vt"][i],
                        row(prm["ln_ffn_g"][i]), row(prm["ln_ffn_b"][i]))
        x = _ple(x, p[i], prm["ple_proj"][i], prm["ple_gate"][i])
    return x.reshape(n_seq, seq, D_MODEL)


def kernel(x_prompt, x_sample, p_prompt, p_sample, a_w_in, a_q_norm, a_kv_norm, a_w_uq, a_w_ukv, a_w_out,
           c_w_in, c_sink, c_w_out, ln_mix_g, ln_mix_b, ln_ffn_g, ln_ffn_b, peer_w_q, peer_sub_keys,
           peer_u, peer_v, ple_proj, ple_gate):
    prm = {
        "even": _prep_even(a_w_in[0], a_q_norm[0], a_kv_norm[0], a_w_uq[0], a_w_ukv[0], a_w_out[0]),
        "odd": _prep_odd(c_w_in[0], c_w_out[0]),
        "sink": c_sink[0],
        "ln_mix_g": ln_mix_g, "ln_mix_b": ln_mix_b, "ln_ffn_g": ln_ffn_g, "ln_ffn_b": ln_ffn_b,
        "peer_wq_t": jnp.swapaxes(peer_w_q, 1, 2),
        "peer_keys": peer_sub_keys,
        "peer_u": peer_u.astype(BF16),
        "peer_vt": jnp.swapaxes(peer_v, 1, 2).astype(BF16),
        "ple_proj": ple_proj.astype(BF16),
        "ple_gate": ple_gate.astype(BF16),
    }
    y_prompt = _trunk(x_prompt, p_prompt, prm)
    y_sample = _trunk(x_sample, p_sample, prm)
    return (y_prompt, y_sample)
```

```python
import functools

import jax
import jax.numpy as jnp
import numpy as np
from jax import lax
from jax.experimental import pallas as pl
from jax.experimental.pallas import tpu as pltpu

F32 = jnp.float32
BF16 = jnp.bfloat16

D_MODEL = 1024
DEPTH = 2
PLE_DIM = 256
MLA_HEADS = 8
MLA_Q_LORA = 384
MLA_KV_LORA = 256
MLA_NOPE = 64
MLA_ROPE = 32
MLA_V = 64
MLA_THETA = 10000.0
MLA_SLOT = 128
DIL_HEADS = 8
DIL_HD = 64
DIL_PATTERNS = ((128, 1), (512, 4), (2048, 16))
C_HEADS = 16
C_KV_HEADS = 4
C_HD = 64
C_RADIUS = 128
ROPE_THETA = 500000.0
ROPE_FRACTION = 4
PEER_HEADS = 8
PEER_N_KEYS = 128
PEER_N_EXPERTS = PEER_N_KEYS * PEER_N_KEYS
PEER_D_KEY = 128
PEER_TOPK = 16
DN_ALPHA = (2 * DEPTH) ** 0.25
LN_EPS = 1e-5
RMS_EPS = 1e-6
NEG = -1e30
INV_SQRT2 = 0.7071067811865476
NOT_RANKED = 100.0

LANES = 128
SUBLANES = 8
BF16_ROWS = 16
VMEM_LIMIT_BYTES = 56 * 1024 * 1024

TM = 512
TQ_MLA = 512
TK_MLA = 1024
T_BAND = 128
TM_SEL = 256
TM_PEER = 512
EB_PEER = 1024

_NT = (((1,), (1,)), ((), ()))


def _params(*sem):
    return pltpu.CompilerParams(dimension_semantics=sem, vmem_limit_bytes=VMEM_LIMIT_BYTES)


def _layer_norm(y, g, b):
    mu = jnp.mean(y, axis=-1, keepdims=True)
    d = y - mu
    var = jnp.mean(d * d, axis=-1, keepdims=True)
    return d * lax.rsqrt(var + LN_EPS) * g + b


def _rms_norm(y, g):
    return y * lax.rsqrt(jnp.mean(y * y, axis=-1, keepdims=True) + RMS_EPS) * g


def _tile_lanes(t, reps):
    return jnp.concatenate([t] * reps, axis=-1)


def _low_half_lanes():
    return lax.broadcasted_iota(jnp.int32, (1, LANES), 1) < LANES // 2


def _const_spec(shape):
    nd = len(shape)
    return pl.BlockSpec(shape, lambda *_: (0,) * nd)


def _inproj_even_kernel(x_ref, wcq_ref, wckv_ref, wkr_ref, wqb_ref, wkb_ref, wvb_ref,
                        qn_ref, kvn_ref, wuq_ref, wk_ref, wv_ref, ta_ref, tb_ref,
                        qa_ref, ka_ref, va_ref, qb_ref, kb_ref, vb_ref):
    xb = x_ref[...].astype(BF16)
    dot = functools.partial(jnp.dot, preferred_element_type=F32)
    cq = _rms_norm(dot(xb, wcq_ref[...]), qn_ref[...]).astype(BF16)
    q12 = dot(cq, wuq_ref[...])
    cq_t = _tile_lanes(ta_ref[0], MLA_HEADS)
    sq_t = _tile_lanes(ta_ref[1], MLA_HEADS)
    qa_ref[...] = (q12[:, :1024] * cq_t + q12[:, 1024:] * sq_t).astype(BF16)
    ckv = _rms_norm(dot(xb, wckv_ref[...]), kvn_ref[...]).astype(BF16)
    kr12 = dot(xb, wkr_ref[...])
    kr = kr12[:, :LANES] * ta_ref[2] + kr12[:, LANES:] * ta_ref[3]
    ka_ref[...] = (dot(ckv, wk_ref[...]) + _tile_lanes(kr, MLA_HEADS)).astype(BF16)
    va_ref[...] = dot(ckv, wv_ref[...]).astype(BF16)
    q12b = dot(xb, wqb_ref[...])
    qb_ref[...] = (q12b[:, :512] * _tile_lanes(tb_ref[0], 4)
                   + q12b[:, 512:] * _tile_lanes(tb_ref[1], 4)).astype(BF16)
    k12b = dot(xb, wkb_ref[...])
    kb_ref[...] = (k12b[:, :512] * _tile_lanes(tb_ref[2], 4)
                   + k12b[:, 512:] * _tile_lanes(tb_ref[3], 4)).astype(BF16)
    vb_ref[...] = dot(xb, wvb_ref[...]).astype(BF16)


def _inproj_even(x, w, ta, tb, seq):
    n = x.shape[0]
    pos_blocks = seq // TM
    row = lambda i: (i, 0)
    tab = lambda i: (0, i % pos_blocks, 0)
    wnames = ("wcq", "wckv", "wkr", "wqb", "wkb", "wvb", "qn", "kvn", "wuq", "wk", "wv")
    ws = [w[k] for k in wnames]
    out_w = (1024, 1024, 512, 512, 512, 512)
    return pl.pallas_call(
        _inproj_even_kernel,
        grid=(n // TM,),
        in_specs=[pl.BlockSpec((TM, D_MODEL), row)] + [_const_spec(a.shape) for a in ws]
        + [pl.BlockSpec((4, TM, LANES), tab), pl.BlockSpec((4, TM, LANES), tab)],
        out_specs=[pl.BlockSpec((TM, c), row) for c in out_w],
        out_shape=[jax.ShapeDtypeStruct((n, c), BF16) for c in out_w],
        compiler_params=_params("parallel"),
        name="inproj_even",
    )(x, *ws, ta, tb)


def _mla_kernel(q_ref, k_ref, v_ref, o_ref, m_ref, l_ref, acc_ref):
    ki = pl.program_id(2)
    low = _low_half_lanes()
    reps = TK_MLA // LANES

    @pl.when(ki == 0)
    def _init():
        m_ref[...] = jnp.full(m_ref.shape, NEG, F32)
        l_ref[...] = jnp.zeros(l_ref.shape, F32)
        acc_ref[...] = jnp.zeros(acc_ref.shape, F32)

    for j in range(MLA_HEADS // 2):
        vslot = v_ref[:, j * LANES:(j + 1) * LANES]
        prod, alphas = [], []
        for half in range(2):
            h = 2 * j + half
            qh = q_ref[:, h * MLA_SLOT:(h + 1) * MLA_SLOT]
            kh = k_ref[:, h * MLA_SLOT:(h + 1) * MLA_SLOT]
            s = lax.dot_general(qh, kh, _NT, preferred_element_type=F32)
            m_prev = m_ref[h]
            m_new = jnp.maximum(m_prev, jnp.max(s, axis=-1, keepdims=True))
            alpha = jnp.exp(m_prev - m_new)
            p = jnp.exp(s - _tile_lanes(m_new, reps))
            l_ref[h] = alpha * l_ref[h] + jnp.sum(p, axis=-1, keepdims=True)
            m_ref[h] = m_new
            prod.append(jnp.dot(p.astype(BF16), vslot, preferred_element_type=F32))
            alphas.append(alpha)
        acc_ref[j] = (jnp.where(low, alphas[0], alphas[1]) * acc_ref[j]
                      + jnp.where(low, prod[0], prod[1]))

    @pl.when(ki == pl.num_programs(2) - 1)
    def _fin():
        outs = [acc_ref[j] / jnp.where(low, l_ref[2 * j], l_ref[2 * j + 1]) for j in range(MLA_HEADS // 2)]
        o_ref[...] = jnp.concatenate(outs, axis=-1).astype(BF16)


def _mla_attention(q, k, v, n_seq, seq):
    n = q.shape[0]
    nq, nk = seq // TQ_MLA, seq // TK_MLA
    return pl.pallas_call(
        _mla_kernel,
        grid=(n_seq, nq, nk),
        in_specs=[pl.BlockSpec((TQ_MLA, 1024), lambda b, i, j: (b * nq + i, 0)),
                  pl.BlockSpec((TK_MLA, 1024), lambda b, i, j: (b * nk + j, 0)),
                  pl.BlockSpec((TK_MLA, 512), lambda b, i, j: (b * nk + j, 0))],
        out_specs=pl.BlockSpec((TQ_MLA, 512), lambda b, i, j: (b * nq + i, 0)),
        out_shape=jax.ShapeDtypeStruct((n, 512), BF16),
        scratch_shapes=[pltpu.VMEM((MLA_HEADS, TQ_MLA, LANES), F32),
                        pltpu.VMEM((MLA_HEADS, TQ_MLA, LANES), F32),
                        pltpu.VMEM((MLA_HEADS // 2, TQ_MLA, LANES), F32)],
        compiler_params=_params("parallel", "parallel", "arbitrary"),
        name="mla_attention",
    )(q, k, v)


def _band_bias(radius):
    row = np.arange(T_BAND)[:, None]
    col = np.arange(3 * T_BAND)[None, :]
    d = col - T_BAND - row
    return np.where(np.abs(d) <= radius, 0.0, NEG).astype(np.float32)


def _banded_kernel(*refs, n_pairs, slot_of_pair, has_sink, emit_lse):
    refs = list(refs)
    sink_ref = refs.pop(0) if has_sink else None
    q_ref, kp_ref, kc_ref, kn_ref, vp_ref, vc_ref, vn_ref, bias_ref = refs[:8]
    o_ref = refs[8]
    lse_ref = refs[9] if emit_lse else None
    qi = pl.program_id(2)
    nb = pl.num_programs(2)
    col = lax.broadcasted_iota(jnp.int32, (1, 3 * T_BAND), 1)
    no_prev = jnp.where(qi == 0, NEG, 0.0)
    no_next = jnp.where(qi == nb - 1, NEG, 0.0)
    bias = (bias_ref[...] + jnp.where(col < T_BAND, no_prev, 0.0)
            + jnp.where(col >= 2 * T_BAND, no_next, 0.0))
    low = _low_half_lanes()
    for m in range(n_pairs):
        sl = slice(slot_of_pair(m) * LANES, (slot_of_pair(m) + 1) * LANES)
        kcat = jnp.concatenate([kp_ref[:, sl], kc_ref[:, sl], kn_ref[:, sl]], axis=0)
        vcat = jnp.concatenate([vp_ref[:, sl], vc_ref[:, sl], vn_ref[:, sl]], axis=0)
        qp = q_ref[:, m * LANES:(m + 1) * LANES]
        res, lses = [], []
        for half in range(2):
            keep = low if half == 0 else jnp.logical_not(low)
            qm = jnp.where(keep, qp, jnp.zeros_like(qp))
            s = lax.dot_general(qm, kcat, _NT, preferred_element_type=F32) + bias
            mx = jnp.max(s, axis=-1, keepdims=True)
            if has_sink:
                sk = sink_ref[2 * m + half]
                mx = jnp.maximum(mx, sk)
            p = jnp.exp(s - mx)
            den = jnp.sum(p, axis=-1, keepdims=True)
            if has_sink:
                den = den + jnp.exp(sk - mx)
            res.append(jnp.dot(p.astype(BF16), vcat, preferred_element_type=F32) / den)
            if emit_lse:
                lses.append(mx + jnp.log(den))
        o_ref[:, m * LANES:(m + 1) * LANES] = jnp.where(low, res[0], res[1]).astype(BF16)
        if emit_lse:
            lse_ref[:, m * LANES:(m + 1) * LANES] = jnp.where(low, lses[0], lses[1])


def _banded_attention(q, k, v, n_seq, seq, dil, radius, slot_of_pair, sink=None, emit_lse=False):
    n, qw = q.shape
    kw = k.shape[1]
    sub = seq // dil
    nb = sub // T_BAND
    view = lambda t: t.reshape(n // dil, dil * t.shape[1])
    cur = lambda b, r, i: (b * nb + i, r)
    prev = lambda b, r, i: (b * nb + jnp.maximum(i - 1, 0), r)
    nxt = lambda b, r, i: (b * nb + jnp.minimum(i + 1, nb - 1), r)
    kspec = lambda f: pl.BlockSpec((T_BAND, kw), f)
    in_specs = [pl.BlockSpec((T_BAND, qw), cur),
                kspec(prev), kspec(cur), kspec(nxt), kspec(prev), kspec(cur), kspec(nxt),
                pl.BlockSpec((T_BAND, 3 * T_BAND), lambda b, r, i: (0, 0))]
    args = [view(q), view(k), view(k), view(k), view(v), view(v), view(v), jnp.asarray(_band_bias(radius))]
    if sink is not None:
        in_specs = [pl.BlockSpec(memory_space=pltpu.SMEM)] + in_specs
        args = [sink] + args
    out_specs = [pl.BlockSpec((T_BAND, qw), cur)]
    out_shape = [jax.ShapeDtypeStruct((n // dil, dil * qw), BF16)]
    if emit_lse:
        out_specs.append(pl.BlockSpec((T_BAND, qw), cur))
        out_shape.append(jax.ShapeDtypeStruct((n // dil, dil * qw), F32))
    outs = pl.pallas_call(
        functools.partial(_banded_kernel, n_pairs=qw // LANES, slot_of_pair=slot_of_pair,
                          has_sink=sink is not None, emit_lse=emit_lse),
        grid=(n_seq, dil, nb),
        in_specs=in_specs, out_specs=out_specs, out_shape=out_shape,
        compiler_params=_params("parallel", "parallel", "parallel"),
        name="banded_attention",
    )(*args)
    return [o.reshape(n, qw) for o in outs]


def _outproj_even_kernel(oa_ref, o1_ref, o2_ref, o3_ref, l1_ref, l2_ref, l3_ref, wa_ref, wb_ref,
                         x_ref, g_ref, b_ref, y_ref):
    l1, l2, l3 = l1_ref[...], l2_ref[...], l3_ref[...]
    mx = jnp.maximum(jnp.maximum(l1, l2), l3)
    e1, e2, e3 = jnp.exp(l1 - mx), jnp.exp(l2 - mx), jnp.exp(l3 - mx)
    ob = (e1 * o1_ref[...].astype(F32) + e2 * o2_ref[...].astype(F32) + e3 * o3_ref[...].astype(F32)) / (e1 + e2 + e3)
    mix = (jnp.dot(oa_ref[...], wa_ref[...], preferred_element_type=F32)
           + jnp.dot(ob.astype(BF16), wb_ref[...], preferred_element_type=F32))
    y_ref[...] = _layer_norm(DN_ALPHA * x_ref[...] + mix, g_ref[...], b_ref[...])


def _outproj_even(oa, obs, lses, wa, wb, x, g, b):
    n = x.shape[0]
    row = lambda i: (i, 0)
    half = pl.BlockSpec((TM, 512), row)
    return pl.pallas_call(
        _outproj_even_kernel,
        grid=(n // TM,),
        in_specs=[half] * 7 + [_const_spec(wa.shape), _const_spec(wb.shape),
                               pl.BlockSpec((TM, D_MODEL), row), _const_spec(g.shape), _const_spec(b.shape)],
        out_specs=pl.BlockSpec((TM, D_MODEL), row),
        out_shape=jax.ShapeDtypeStruct((n, D_MODEL), F32),
        compiler_params=_params("parallel"),
        name="outproj_even",
    )(oa, *obs, *lses, wa, wb, x, g, b)


def _outproj_odd_kernel(o_ref, w_ref, x_ref, g_ref, b_ref, y_ref):
    mix = jnp.dot(o_ref[...], w_ref[...], preferred_element_type=F32)
    y_ref[...] = _layer_norm(DN_ALPHA * x_ref[...] + mix, g_ref[...], b_ref[...])


def _outproj_odd(o, w, x, g, b):
    n = x.shape[0]
    row = lambda i: (i, 0)
    return pl.pallas_call(
        _outproj_odd_kernel,
        grid=(n // TM,),
        in_specs=[pl.BlockSpec((TM, o.shape[1]), row), _const_spec(w.shape),
                  pl.BlockSpec((TM, D_MODEL), row), _const_spec(g.shape), _const_spec(b.shape)],
        out_specs=pl.BlockSpec((TM, D_MODEL), row),
        out_shape=jax.ShapeDtypeStruct((n, D_MODEL), F32),
        compiler_params=_params("parallel"),
        name="outproj_odd",
    )(o, w, x, g, b)


def _inproj_odd_kernel(x_ref, wq_ref, wk_ref, wv_ref, tb_ref, q_ref, k_ref, v_ref):
    xb = x_ref[...].astype(BF16)
    dot = functools.partial(jnp.dot, preferred_element_type=F32)
    q12 = dot(xb, wq_ref[...])
    q_ref[...] = (q12[:, :1024] * _tile_lanes(tb_ref[0], 8)
                  + q12[:, 1024:] * _tile_lanes(tb_ref[1], 8)).astype(BF16)
    k12 = dot(xb, wk_ref[...])
    k_ref[...] = (k12[:, :512] * _tile_lanes(tb_ref[2], 4)
                  + k12[:, 512:] * _tile_lanes(tb_ref[3], 4)).astype(BF16)
    v_ref[...] = dot(xb, wv_ref[...]).astype(BF16)


def _inproj_odd(x, w, tb, seq):
    n = x.shape[0]
    pos_blocks = seq // TM
    row = lambda i: (i, 0)
    ws = [w["wq"], w["wk"], w["wv"]]
    out_w = (1024, 512, 512)
    return pl.pallas_call(
        _inproj_odd_kernel,
        grid=(n // TM,),
        in_specs=[pl.BlockSpec((TM, D_MODEL), row)] + [_const_spec(a.shape) for a in ws]
        + [pl.BlockSpec((4, TM, LANES), lambda i: (0, i % pos_blocks, 0))],
        out_specs=[pl.BlockSpec((TM, c), row) for c in out_w],
        out_shape=[jax.ShapeDtypeStruct((n, c), BF16) for c in out_w],
        compiler_params=_params("parallel"),
        name="inproj_odd",
    )(x, *ws, tb)


def _extract_ranked(cur, k, out_ref, lanes):
    rank = jnp.full(cur.shape, NOT_RANKED, F32)
    for i in range(k):
        mx = jnp.max(cur, axis=0, keepdims=True)
        out_ref[i:i + 1, lanes] = mx
        hit = cur >= mx
        rank = jnp.where(hit, float(i), rank)
        if i + 1 < k:
            cur = jnp.where(hit, NEG, cur)
    return rank


def _kth_largest(cur, k):
    for i in range(k):
        mx = jnp.max(cur, axis=0, keepdims=True)
        if i + 1 < k:
            cur = jnp.where(cur >= mx, NEG, cur)
    return mx


def _split_bf16(t):
    hi = t.astype(BF16)
    return hi, (t - hi.astype(F32)).astype(BF16)


def _dot3(a_hi, a_lo, b_hi, b_lo, dims):
    d = functools.partial(lax.dot_general, dimension_numbers=dims, preferred_element_type=F32)
    return d(a_hi, b_hi) + (d(a_hi, b_lo) + d(a_lo, b_hi))


def _peer_select_kernel(x_ref, wqh_ref, wql_ref, kh_ref, kl_ref, r2_ref, e2_ref, k1_ref, e1_ref,
                        qh_ref, ql_ref, a_ref, b_ref):
    h = pl.program_id(1)

    @pl.when(h == 0)
    def _project():
        xh, xl = _split_bf16(x_ref[...])
        q = _dot3(wqh_ref[...], wql_ref[...], xh, xl, _NT)
        qh_ref[...], ql_ref[...] = _split_bf16(q)

    half = PEER_D_KEY // 2
    row0 = pl.multiple_of(h * PEER_D_KEY, PEER_D_KEY)
    mm = (((1,), (0,)), ((), ()))
    s1 = _dot3(kh_ref[0, 0], kl_ref[0, 0], qh_ref[pl.ds(row0, half), :], ql_ref[pl.ds(row0, half), :], mm)
    s2 = _dot3(kh_ref[0, 1], kl_ref[0, 1], qh_ref[pl.ds(row0 + half, half), :],
               ql_ref[pl.ds(row0 + half, half), :], mm)
    for lt in range(TM_SEL // LANES):
        ls = slice(lt * LANES, (lt + 1) * LANES)
        s1t, s2t = s1[:, ls], s2[:, ls]
        rank1 = _extract_ranked(s1t, PEER_TOPK, a_ref, ls)
        rank2 = _extract_ranked(s2t, PEER_TOPK, b_ref, ls)
        a16, b16 = a_ref[:, ls], b_ref[:, ls]
        a0, b0 = a16[0:1], b16[0:1]
        cand = [a0 + b16]
        cand += [a16[i:i + 1] + b16[0:8] for i in range(1, 8)]
        cand += [a16[8:16] + b0]
        cand = jnp.concatenate(cand, axis=0)
        v16 = _kth_largest(cand, PEER_TOPK)
        z = jnp.sum(jnp.where(cand >= v16, jnp.exp(cand - (a0 + b0)), 0.0), axis=0, keepdims=True)
        kcount = jnp.zeros(s1t.shape, F32)
        for i in range(PEER_TOPK):
            cnt = jnp.sum(jnp.where(a16[i:i + 1] + b16 >= v16, 1.0, 0.0), axis=0, keepdims=True)
            kcount = jnp.where(rank1 == float(i), cnt, kcount)
        r2_ref[:, ls] = rank2.astype(BF16)
        e2_ref[:, ls] = (jnp.exp(s2t - b0) * (0.5 / z)).astype(BF16)
        k1_ref[:, ls] = kcount
        e1_ref[:, ls] = jnp.exp(s1t - a0)


def _peer_select(x, wq_hi, wq_lo, keys_hi, keys_lo):
    n = x.shape[0]
    spec = pl.BlockSpec((PEER_N_KEYS, TM_SEL), lambda i, h: (h, i))
    shape = lambda dt: jax.ShapeDtypeStruct((PEER_HEADS * PEER_N_KEYS, n), dt)
    kspec = pl.BlockSpec((1, 2, PEER_N_KEYS, PEER_D_KEY // 2), lambda i, h: (h, 0, 0, 0))
    return pl.pallas_call(
        _peer_select_kernel,
        grid=(n // TM_SEL, PEER_HEADS),
        in_specs=[pl.BlockSpec((TM_SEL, D_MODEL), lambda i, h: (i, 0)),
                  _const_spec(wq_hi.shape), _const_spec(wq_lo.shape), kspec, kspec],
        out_specs=[spec] * 4,
        out_shape=[shape(BF16), shape(BF16), shape(F32), shape(F32)],
        scratch_shapes=[pltpu.VMEM((PEER_HEADS * PEER_D_KEY, TM_SEL), BF16),
                        pltpu.VMEM((PEER_HEADS * PEER_D_KEY, TM_SEL), BF16),
                        pltpu.VMEM((PEER_TOPK, TM_SEL), F32),
                        pltpu.VMEM((PEER_TOPK, TM_SEL), F32)],
        compiler_params=_params("parallel", "arbitrary"),
        name="peer_select",
    )(x, wq_hi, wq_lo, keys_hi, keys_lo)


def _peer_expert_kernel(x_ref, r2_ref, e2_ref, k1_ref, e1_ref, u_ref, vt_ref, g_ref, b_ref, y_ref,
                        xb_ref, a_ref, h_ref, acc_ref, kb_ref, e1b_ref, r2s_ref, e2s_ref):
    eb = pl.program_id(1)
    n_i1 = EB_PEER // PEER_N_KEYS
    n_st = PEER_N_KEYS // BF16_ROWS
    halves = 2
    rows_half = EB_PEER // halves

    @pl.when(eb == 0)
    def _init():
        xb_ref[...] = x_ref[...].astype(BF16)
        acc_ref[...] = jnp.zeros(acc_ref.shape, F32)
        r2s_ref[...] = r2_ref[...]
        e2s_ref[...] = e2_ref[...]

    for hh in range(halves):
        rs = slice(hh * rows_half, (hh + 1) * rows_half)
        a_ref[rs, :] = lax.dot_general(u_ref[rs, :], xb_ref[...], _NT, preferred_element_type=F32)
    i1_base = pl.multiple_of(eb * n_i1, SUBLANES)
    tile_rows = lambda row: jnp.broadcast_to(row, (BF16_ROWS, TM_PEER))
    for h in range(PEER_HEADS):
        hs = slice(h * PEER_N_KEYS, (h + 1) * PEER_N_KEYS)
        k8 = k1_ref[pl.ds(h * PEER_N_KEYS + i1_base, n_i1), :]
        e8 = e1_ref[pl.ds(h * PEER_N_KEYS + i1_base, n_i1), :]
        kb_ref[hs, :] = jnp.concatenate([tile_rows(k8[r:r + 1]) for r in range(n_i1)], axis=0).astype(BF16)
        e1b_ref[hs, :] = jnp.concatenate([tile_rows(e8[r:r + 1]) for r in range(n_i1)], axis=0).astype(BF16)
    zero = jnp.zeros((PEER_N_KEYS, LANES), BF16)
    rep = lambda tile: jnp.concatenate([tile] * n_st, axis=0)
    for hh in range(halves):
        for lt in range(TM_PEER // LANES):
            ls = slice(lt * LANES, (lt + 1) * LANES)
            for rp in range(n_i1 // halves // 2):
                r0 = hh * (n_i1 // halves) + 2 * rp
                gate = [zero, zero]
                for h in range(PEER_HEADS):
                    hs = slice(h * PEER_N_KEYS, (h + 1) * PEER_N_KEYS)
                    pair = slice(h * PEER_N_KEYS + r0 * BF16_ROWS, h * PEER_N_KEYS + (r0 + 2) * BF16_ROWS)
                    kb2, e1b2 = kb_ref[pair, ls], e1b_ref[pair, ls]
                    r2s, e2s = r2s_ref[hs, ls], e2s_ref[hs, ls]
                    for t in range(2):
                        ts = slice(t * BF16_ROWS, (t + 1) * BF16_ROWS)
                        gate[t] = gate[t] + jnp.where(r2s < rep(kb2[ts]), e2s, zero) * rep(e1b2[ts])
                for t in range(2):
                    rows = slice((r0 + t) * PEER_N_KEYS, (r0 + t + 1) * PEER_N_KEYS)
                    a = a_ref[rows, ls]
                    act = (a * (1.0 + lax.erf(a * INV_SQRT2))).astype(BF16)
                    h_ref[rows, ls] = act * gate[t]
        rs = slice(hh * rows_half, (hh + 1) * rows_half)
        acc_ref[...] += jnp.dot(vt_ref[:, rs], h_ref[rs, :], preferred_element_type=F32)

    @pl.when(eb == pl.num_programs(1) - 1)
    def _fin():
        y = DN_ALPHA * x_ref[...] + acc_ref[...].T
        y_ref[...] = _layer_norm(y, g_ref[...], b_ref[...])


def _peer_experts(x, stats, u_bf, vt_bf, g, b):
    n = x.shape[0]
    stat_spec = pl.BlockSpec((PEER_HEADS * PEER_N_KEYS, TM_PEER), lambda i, e: (0, i))
    return pl.pallas_call(
        _peer_expert_kernel,
        grid=(n // TM_PEER, PEER_N_EXPERTS // EB_PEER),
        in_specs=[pl.BlockSpec((TM_PEER, D_MODEL), lambda i, e: (i, 0))] + [stat_spec] * 4
        + [pl.BlockSpec((EB_PEER, D_MODEL), lambda i, e: (e, 0)),
           pl.BlockSpec((D_MODEL, EB_PEER), lambda i, e: (0, e)),
           _const_spec(g.shape), _const_spec(b.shape)],
        out_specs=pl.BlockSpec((TM_PEER, D_MODEL), lambda i, e: (i, 0)),
        out_shape=jax.ShapeDtypeStruct((n, D_MODEL), F32),
        scratch_shapes=[pltpu.VMEM((TM_PEER, D_MODEL), BF16),
                        pltpu.VMEM((EB_PEER, TM_PEER), F32),
                        pltpu.VMEM((EB_PEER, TM_PEER), BF16),
                        pltpu.VMEM((D_MODEL, TM_PEER), F32),
                        pltpu.VMEM((PEER_HEADS * (EB_PEER // PEER_N_KEYS) * BF16_ROWS, TM_PEER), BF16),
                        pltpu.VMEM((PEER_HEADS * (EB_PEER // PEER_N_KEYS) * BF16_ROWS, TM_PEER), BF16),
                        pltpu.VMEM((PEER_HEADS * PEER_N_KEYS, TM_PEER), BF16),
                        pltpu.VMEM((PEER_HEADS * PEER_N_KEYS, TM_PEER), BF16)],
        compiler_params=_params("parallel", "arbitrary"),
        name="peer_experts",
    )(x, *stats, u_bf, vt_bf, g, b)


def _ple_kernel(x_ref, p_ref, wp_ref, wg_ref, y_ref):
    x = x_ref[...]
    proj = jnp.dot(p_ref[...].astype(BF16), wp_ref[...], preferred_element_type=F32)
    gate = jax.nn.sigmoid(jnp.dot(x.astype(BF16), wg_ref[...], preferred_element_type=F32))
    y_ref[...] = x + proj * gate


def _ple(x, p, wp, wg):
    n = x.shape[0]
    row = lambda i: (i, 0)
    return pl.pallas_call(
        _ple_kernel,
        grid=(n // TM,),
        in_specs=[pl.BlockSpec((TM, D_MODEL), row), pl.BlockSpec((TM, PLE_DIM), row),
                  _const_spec(wp.shape), _const_spec(wg.shape)],
        out_specs=pl.BlockSpec((TM, D_MODEL), row),
        out_shape=jax.ShapeDtypeStruct((n, D_MODEL), F32),
        compiler_params=_params("parallel"),
        name="ple_gate",
    )(x, p, wp, wg)


def _rope_partner_cols(w, head_dim, half):
    n_heads = w.shape[1] // head_dim
    w3 = w.reshape(w.shape[0], n_heads, head_dim)
    sw = jnp.concatenate([w3[..., half:2 * half], w3[..., :half],
                          jnp.zeros_like(w3[..., 2 * half:])], axis=-1)
    return sw.reshape(w.shape)


def _twice_per_head(w, head_dim):
    n_heads = w.shape[1] // head_dim
    w3 = w.reshape(w.shape[0], n_heads, head_dim)
    return jnp.concatenate([w3, w3], axis=-1).reshape(w.shape[0], 2 * w.shape[1])


def _rope_tables(seq, half, theta, lanes_before, lanes_after, pass_through, scale):
    inv = theta ** (-jnp.arange(half, dtype=F32) / half)
    ang = jnp.arange(seq).astype(F32)[:, None] * inv[None, :]
    cos, sin = jnp.cos(ang), jnp.sin(ang)
    ones = lambda w, v: jnp.full((seq, w), v, F32)
    ct = jnp.concatenate([ones(lanes_before, pass_through), cos, cos, ones(lanes_after, pass_through)], axis=-1)
    st = jnp.concatenate([ones(lanes_before, 0.0), -sin, sin, ones(lanes_after, 0.0)], axis=-1)
    return ct * scale, st * scale


def _prep_even(a_w_in, a_q_norm, a_kv_norm, a_w_uq, a_w_ukv, a_w_out):
    o1 = MLA_Q_LORA
    o2 = o1 + MLA_KV_LORA
    o3 = o2 + MLA_ROPE
    nb = DIL_HEADS * DIL_HD
    bf = lambda t: t.astype(BF16)
    w = {}
    w["wcq"] = bf(a_w_in[:, :o1])
    w["wckv"] = bf(a_w_in[:, o1:o2])
    kr = a_w_in[:, o2:o3]
    krs = jnp.concatenate([kr[:, MLA_ROPE // 2:], kr[:, :MLA_ROPE // 2]], axis=-1)
    pad_slot = lambda t: jnp.pad(t, ((0, 0), (MLA_NOPE, MLA_SLOT - MLA_NOPE - MLA_ROPE)))
    w["wkr"] = bf(jnp.concatenate([pad_slot(kr), pad_slot(krs)], axis=-1))
    qb = a_w_in[:, o3:o3 + nb]
    kb = a_w_in[:, o3 + nb:o3 + 2 * nb]
    rd2 = DIL_HD // ROPE_FRACTION // 2
    w["wqb"] = bf(jnp.concatenate([qb, _rope_partner_cols(qb, DIL_HD, rd2)], axis=-1))
    w["wkb"] = bf(jnp.concatenate([kb, _rope_partner_cols(kb, DIL_HD, rd2)], axis=-1))
    w["wvb"] = bf(a_w_in[:, o3 + 2 * nb:])
    w["qn"] = a_q_norm.reshape(1, -1)
    w["kvn"] = a_kv_norm.reshape(1, -1)
    uq = a_w_uq.reshape(MLA_Q_LORA, MLA_HEADS, MLA_NOPE + MLA_ROPE)
    zpad = jnp.zeros((MLA_Q_LORA, MLA_HEADS, MLA_SLOT - MLA_NOPE - MLA_ROPE), F32)
    uq1 = jnp.concatenate([uq, zpad], axis=-1).reshape(MLA_Q_LORA, MLA_HEADS * MLA_SLOT)
    r = uq[..., MLA_NOPE:]
    uq2 = jnp.concatenate([jnp.zeros_like(uq[..., :MLA_NOPE]), r[..., MLA_ROPE // 2:], r[..., :MLA_ROPE // 2],
                           zpad], axis=-1).reshape(MLA_Q_LORA, MLA_HEADS * MLA_SLOT)
    w["wuq"] = bf(jnp.concatenate([uq1, uq2], axis=-1))
    ukv = a_w_ukv.reshape(MLA_KV_LORA, MLA_HEADS, MLA_NOPE + MLA_V)
    kpad = jnp.zeros((MLA_KV_LORA, MLA_HEADS, MLA_SLOT - MLA_NOPE), F32)
    w["wk"] = bf(jnp.concatenate([ukv[..., :MLA_NOPE], kpad], axis=-1).reshape(MLA_KV_LORA, -1))
    w["wv"] = bf(ukv[..., MLA_NOPE:].reshape(MLA_KV_LORA, MLA_HEADS * MLA_V))
    w["wout_a"] = bf(a_w_out[:MLA_HEADS * MLA_V])
    w["wout_b"] = bf(a_w_out[MLA_HEADS * MLA_V:])
    return w


def _prep_odd(c_w_in, c_w_out):
    e1 = C_HEADS * C_HD
    e2 = e1 + C_KV_HEADS * C_HD
    rd2 = C_HD // ROPE_FRACTION // 2
    q, k = c_w_in[:, :e1], c_w_in[:, e1:e2]
    k2 = lambda t: _twice_per_head(t, C_HD)
    return {
        "wq": jnp.concatenate([q, _rope_partner_cols(q, C_HD, rd2)], axis=-1).astype(BF16),
        "wk": jnp.concatenate([k2(k), k2(_rope_partner_cols(k, C_HD, rd2))], axis=-1).astype(BF16),
        "wv": k2(c_w_in[:, e2:]).astype(BF16),
        "wout": c_w_out.astype(BF16),
    }


def _split_kernel(w_ref, hi_ref, lo_ref):
    hi_ref[...], lo_ref[...] = _split_bf16(w_ref[...])


def _split_hi_lo(w):
    flat = w.reshape(-1, LANES)
    rows = flat.shape[0]
    blk = min(rows, 1024)
    spec = pl.BlockSpec((blk, LANES), lambda i: (i, 0))
    hi, lo = pl.pallas_call(
        _split_kernel,
        grid=(rows // blk,),
        in_specs=[spec], out_specs=[spec, spec],
        out_shape=[jax.ShapeDtypeStruct(flat.shape, BF16)] * 2,
        compiler_params=_params("parallel"),
        name="split_hi_lo",
    )(flat)
    return hi.reshape(w.shape), lo.reshape(w.shape)


def _prep_peer(peer_w_q, peer_sub_keys, peer_u, peer_v):
    wq_hi, wq_lo = _split_hi_lo(jnp.swapaxes(peer_w_q, 1, 2))
    keys_hi, keys_lo = _split_hi_lo(peer_sub_keys)
    return {
        "wq_hi": wq_hi, "wq_lo": wq_lo, "keys_hi": keys_hi, "keys_lo": keys_lo,
        "u": peer_u.astype(BF16), "vt": jnp.swapaxes(peer_v, 1, 2).astype(BF16),
    }


def _tables_mla(seq):
    scale = (MLA_NOPE + MLA_ROPE) ** -0.5
    pad = MLA_SLOT - MLA_NOPE - MLA_ROPE
    cq, sq = _rope_tables(seq, MLA_ROPE // 2, MLA_THETA, MLA_NOPE, pad, 1.0, scale)
    ck, sk = _rope_tables(seq, MLA_ROPE // 2, MLA_THETA, MLA_NOPE, pad, 0.0, 1.0)
    return jnp.stack([cq, sq, ck, sk])


def _tables_partial(seq, head_dim, q_scale):
    half = head_dim // ROPE_FRACTION // 2
    cq, sq = _rope_tables(seq, half, ROPE_THETA, 0, head_dim - 2 * half, 1.0, q_scale)
    ck, sk = _rope_tables(seq, half, ROPE_THETA, 0, head_dim - 2 * half, 1.0, 1.0)
    two = lambda t: jnp.concatenate([t, t], axis=-1)
    return jnp.stack([two(cq), two(sq), two(ck), two(sk)])


def _peer_layer(x, pw, i, g, b):
    stats = _peer_select(x, pw["wq_hi"][i], pw["wq_lo"][i], pw["keys_hi"][i], pw["keys_lo"][i])
    return _peer_experts(x, stats, pw["u"][i], pw["vt"][i], g, b)


def _mixer_even(x, w, n_seq, seq, g, b):
    qa, ka, va, qb, kb, vb = _inproj_even(x, w, _tables_mla(seq), _tables_partial(seq, DIL_HD, DIL_HD ** -0.5), seq)
    oa = _mla_attention(qa, ka, va, n_seq, seq)
    obs, lses = [], []
    for window, dil in DIL_PATTERNS:
        o, lse = _banded_attention(qb, kb, vb, n_seq, seq, dil, (window // 2) // dil, lambda m: m, emit_lse=True)
        obs.append(o)
        lses.append(lse)
    return _outproj_even(oa, obs, lses, w["wout_a"], w["wout_b"], x, g, b)


def _mixer_odd(x, w, sink, n_seq, seq, g, b):
    q, k, v = _inproj_odd(x, w, _tables_partial(seq, C_HD, C_HD ** -0.5), seq)
    pairs_per_group = C_HEADS // C_KV_HEADS // 2
    (o,) = _banded_attention(q, k, v, n_seq, seq, 1, C_RADIUS, lambda m: m // pairs_per_group, sink=sink)
    return _outproj_odd(o, w["wout"], x, g, b)


def _trunk(x3, p4, prm):
    n_seq, seq, _ = x3.shape
    x = x3.reshape(n_seq * seq, D_MODEL)
    p = p4.reshape(DEPTH, n_seq * seq, PLE_DIM)
    row = lambda t: t.reshape(1, -1)
    for i in range(DEPTH):
        g, b = row(prm["ln_mix_g"][i]), row(prm["ln_mix_b"][i])
        if i % 2 == 0:
            x = _mixer_even(x, prm["even"], n_seq, seq, g, b)
        else:
            x = _mixer_odd(x, prm["odd"], prm["sink"], n_seq, seq, g, b)
        x = _peer_layer(x, prm["peer"], i, row(prm["ln_ffn_g"][i]), row(prm["ln_ffn_b"][i]))
        x = _ple(x, p[i], prm["ple_proj"][i], prm["ple_gate"][i])
    return x.reshape(n_seq, seq, D_MODEL)


def kernel(x_prompt, x_sample, p_prompt, p_sample, a_w_in, a_q_norm, a_kv_norm, a_w_uq, a_w_ukv, a_w_out,
           c_w_in, c_sink, c_w_out, ln_mix_g, ln_mix_b, ln_ffn_g, ln_ffn_b, peer_w_q, peer_sub_keys,
           peer_u, peer_v, ple_proj, ple_gate):
    prm = {
        "even": _prep_even(a_w_in[0], a_q_norm[0], a_kv_norm[0], a_w_uq[0], a_w_ukv[0], a_w_out[0]),
        "odd": _prep_odd(c_w_in[0], c_w_out[0]),
        "sink": c_sink[0],
        "ln_mix_g": ln_mix_g, "ln_mix_b": ln_mix_b, "ln_ffn_g": ln_ffn_g, "ln_ffn_b": ln_ffn_b,
        "peer": _prep_peer(peer_w_q, peer_sub_keys, peer_u, peer_v),
        "ple_proj": ple_proj.astype(BF16),
        "ple_gate": ple_gate.astype(BF16),
    }
    y_prompt = _trunk(x_prompt, p_prompt, prm)
    y_sample = _trunk(x_sample, p_sample, prm)
    return (y_prompt, y_sample)
```

```python
import functools

import jax
import jax.numpy as jnp
import numpy as np
from jax import lax
from jax.experimental import pallas as pl
from jax.experimental.pallas import tpu as pltpu

F32 = jnp.float32
BF16 = jnp.bfloat16

D_MODEL = 1024
DEPTH = 2
PLE_DIM = 256
MLA_HEADS = 8
MLA_Q_LORA = 384
MLA_KV_LORA = 256
MLA_NOPE = 64
MLA_ROPE = 32
MLA_V = 64
MLA_THETA = 10000.0
MLA_SLOT = 128
DIL_HEADS = 8
DIL_HD = 64
DIL_PATTERNS = ((128, 1), (512, 4), (2048, 16))
C_HEADS = 16
C_KV_HEADS = 4
C_HD = 64
C_RADIUS = 128
ROPE_THETA = 500000.0
ROPE_FRACTION = 4
PEER_HEADS = 8
PEER_N_KEYS = 128
PEER_N_EXPERTS = PEER_N_KEYS * PEER_N_KEYS
PEER_D_KEY = 128
PEER_TOPK = 16
DN_ALPHA = (2 * DEPTH) ** 0.25
LN_EPS = 1e-5
RMS_EPS = 1e-6
NEG = -1e30
INV_SQRT2 = 0.7071067811865476
NO_KEY_QUALIFIES = 1.0

LANES = 128
SUBLANES = 8
BF16_ROWS = 16
VMEM_LIMIT_BYTES = 56 * 1024 * 1024

TM = 512
TQ_MLA = 512
TK_MLA = 1024
T_BAND = 128
TM_SEL = 512
TM_PEER = 512
EB_PEER = 1024

_NT = (((1,), (1,)), ((), ()))


def _params(*sem):
    return pltpu.CompilerParams(dimension_semantics=sem, vmem_limit_bytes=VMEM_LIMIT_BYTES)


def _layer_norm(y, g, b):
    mu = jnp.mean(y, axis=-1, keepdims=True)
    d = y - mu
    var = jnp.mean(d * d, axis=-1, keepdims=True)
    return d * lax.rsqrt(var + LN_EPS) * g + b


def _rms_norm(y, g):
    return y * lax.rsqrt(jnp.mean(y * y, axis=-1, keepdims=True) + RMS_EPS) * g


def _tile_lanes(t, reps):
    return jnp.concatenate([t] * reps, axis=-1)


def _low_half_lanes():
    return lax.broadcasted_iota(jnp.int32, (1, LANES), 1) < LANES // 2


def _const_spec(shape):
    nd = len(shape)
    return pl.BlockSpec(shape, lambda *_: (0,) * nd)


def _inproj_even_kernel(x_ref, wcq_ref, wckv_ref, wkr_ref, wqb_ref, wkb_ref, wvb_ref,
                        qn_ref, kvn_ref, wuq_ref, wk_ref, wv_ref, ta_ref, tb_ref,
                        qa_ref, ka_ref, va_ref, qb_ref, kb_ref, vb_ref):
    xb = x_ref[...].astype(BF16)
    dot = functools.partial(jnp.dot, preferred_element_type=F32)
    cq = _rms_norm(dot(xb, wcq_ref[...]), qn_ref[...]).astype(BF16)
    q12 = dot(cq, wuq_ref[...])
    cq_t = _tile_lanes(ta_ref[0], MLA_HEADS)
    sq_t = _tile_lanes(ta_ref[1], MLA_HEADS)
    qa_ref[...] = (q12[:, :1024] * cq_t + q12[:, 1024:] * sq_t).astype(BF16)
    ckv = _rms_norm(dot(xb, wckv_ref[...]), kvn_ref[...]).astype(BF16)
    kr12 = dot(xb, wkr_ref[...])
    kr = kr12[:, :LANES] * ta_ref[2] + kr12[:, LANES:] * ta_ref[3]
    ka_ref[...] = (dot(ckv, wk_ref[...]) + _tile_lanes(kr, MLA_HEADS)).astype(BF16)
    va_ref[...] = dot(ckv, wv_ref[...]).astype(BF16)
    q12b = dot(xb, wqb_ref[...])
    qb_ref[...] = (q12b[:, :512] * _tile_lanes(tb_ref[0], 4)
                   + q12b[:, 512:] * _tile_lanes(tb_ref[1], 4)).astype(BF16)
    k12b = dot(xb, wkb_ref[...])
    kb_ref[...] = (k12b[:, :512] * _tile_lanes(tb_ref[2], 4)
                   + k12b[:, 512:] * _tile_lanes(tb_ref[3], 4)).astype(BF16)
    vb_ref[...] = dot(xb, wvb_ref[...]).astype(BF16)


def _inproj_even(x, w, ta, tb, seq):
    n = x.shape[0]
    pos_blocks = seq // TM
    row = lambda i: (i, 0)
    tab = lambda i: (0, i % pos_blocks, 0)
    wnames = ("wcq", "wckv", "wkr", "wqb", "wkb", "wvb", "qn", "kvn", "wuq", "wk", "wv")
    ws = [w[k] for k in wnames]
    out_w = (1024, 1024, 512, 512, 512, 512)
    return pl.pallas_call(
        _inproj_even_kernel,
        grid=(n // TM,),
        in_specs=[pl.BlockSpec((TM, D_MODEL), row)] + [_const_spec(a.shape) for a in ws]
        + [pl.BlockSpec((4, TM, LANES), tab), pl.BlockSpec((4, TM, LANES), tab)],
        out_specs=[pl.BlockSpec((TM, c), row) for c in out_w],
        out_shape=[jax.ShapeDtypeStruct((n, c), BF16) for c in out_w],
        compiler_params=_params("parallel"),
        name="inproj_even",
    )(x, *ws, ta, tb)


def _mla_kernel(q_ref, k_ref, v_ref, o_ref, m_ref, l_ref, acc_ref):
    ki = pl.program_id(2)
    low = _low_half_lanes()
    reps = TK_MLA // LANES

    @pl.when(ki == 0)
    def _init():
        m_ref[...] = jnp.full(m_ref.shape, NEG, F32)
        l_ref[...] = jnp.zeros(l_ref.shape, F32)
        acc_ref[...] = jnp.zeros(acc_ref.shape, F32)

    for j in range(MLA_HEADS // 2):
        vslot = v_ref[:, j * LANES:(j + 1) * LANES]
        prod, alphas = [], []
        for half in range(2):
            h = 2 * j + half
            qh = q_ref[:, h * MLA_SLOT:(h + 1) * MLA_SLOT]
            kh = k_ref[:, h * MLA_SLOT:(h + 1) * MLA_SLOT]
            s = lax.dot_general(qh, kh, _NT, preferred_element_type=F32)
            m_prev = m_ref[h]
            m_new = jnp.maximum(m_prev, jnp.max(s, axis=-1, keepdims=True))
            alpha = jnp.exp(m_prev - m_new)
            p = jnp.exp(s - _tile_lanes(m_new, reps))
            l_ref[h] = alpha * l_ref[h] + jnp.sum(p, axis=-1, keepdims=True)
            m_ref[h] = m_new
            prod.append(jnp.dot(p.astype(BF16), vslot, preferred_element_type=F32))
            alphas.append(alpha)
        acc_ref[j] = (jnp.where(low, alphas[0], alphas[1]) * acc_ref[j]
                      + jnp.where(low, prod[0], prod[1]))

    @pl.when(ki == pl.num_programs(2) - 1)
    def _fin():
        outs = [acc_ref[j] / jnp.where(low, l_ref[2 * j], l_ref[2 * j + 1]) for j in range(MLA_HEADS // 2)]
        o_ref[...] = jnp.concatenate(outs, axis=-1).astype(BF16)


def _mla_attention(q, k, v, n_seq, seq):
    n = q.shape[0]
    nq, nk = seq // TQ_MLA, seq // TK_MLA
    return pl.pallas_call(
        _mla_kernel,
        grid=(n_seq, nq, nk),
        in_specs=[pl.BlockSpec((TQ_MLA, 1024), lambda b, i, j: (b * nq + i, 0)),
                  pl.BlockSpec((TK_MLA, 1024), lambda b, i, j: (b * nk + j, 0)),
                  pl.BlockSpec((TK_MLA, 512), lambda b, i, j: (b * nk + j, 0))],
        out_specs=pl.BlockSpec((TQ_MLA, 512), lambda b, i, j: (b * nq + i, 0)),
        out_shape=jax.ShapeDtypeStruct((n, 512), BF16),
        scratch_shapes=[pltpu.VMEM((MLA_HEADS, TQ_MLA, LANES), F32),
                        pltpu.VMEM((MLA_HEADS, TQ_MLA, LANES), F32),
                        pltpu.VMEM((MLA_HEADS // 2, TQ_MLA, LANES), F32)],
        compiler_params=_params("parallel", "parallel", "arbitrary"),
        name="mla_attention",
    )(q, k, v)


def _band_bias(radius):
    row = np.arange(T_BAND)[:, None]
    col = np.arange(3 * T_BAND)[None, :]
    d = col - T_BAND - row
    return np.where(np.abs(d) <= radius, 0.0, NEG).astype(np.float32)


def _banded_kernel(*refs, n_pairs, slot_of_pair, has_sink, emit_lse):
    refs = list(refs)
    sink_ref = refs.pop(0) if has_sink else None
    q_ref, kp_ref, kc_ref, kn_ref, vp_ref, vc_ref, vn_ref, bias_ref = refs[:8]
    o_ref = refs[8]
    lse_ref = refs[9] if emit_lse else None
    qi = pl.program_id(2)
    nb = pl.num_programs(2)
    col = lax.broadcasted_iota(jnp.int32, (1, 3 * T_BAND), 1)
    no_prev = jnp.where(qi == 0, NEG, 0.0)
    no_next = jnp.where(qi == nb - 1, NEG, 0.0)
    bias = (bias_ref[...] + jnp.where(col < T_BAND, no_prev, 0.0)
            + jnp.where(col >= 2 * T_BAND, no_next, 0.0))
    low = _low_half_lanes()
    for m in range(n_pairs):
        sl = slice(slot_of_pair(m) * LANES, (slot_of_pair(m) + 1) * LANES)
        kcat = jnp.concatenate([kp_ref[:, sl], kc_ref[:, sl], kn_ref[:, sl]], axis=0)
        vcat = jnp.concatenate([vp_ref[:, sl], vc_ref[:, sl], vn_ref[:, sl]], axis=0)
        qp = q_ref[:, m * LANES:(m + 1) * LANES]
        res, lses = [], []
        for half in range(2):
            keep = low if half == 0 else jnp.logical_not(low)
            qm = jnp.where(keep, qp, jnp.zeros_like(qp))
            s = lax.dot_general(qm, kcat, _NT, preferred_element_type=F32) + bias
            mx = jnp.max(s, axis=-1, keepdims=True)
            if has_sink:
                sk = sink_ref[2 * m + half]
                mx = jnp.maximum(mx, sk)
            p = jnp.exp(s - mx)
            den = jnp.sum(p, axis=-1, keepdims=True)
            if has_sink:
                den = den + jnp.exp(sk - mx)
            res.append(jnp.dot(p.astype(BF16), vcat, preferred_element_type=F32) / den)
            if emit_lse:
                lses.append(mx + jnp.log(den))
        o_ref[:, m * LANES:(m + 1) * LANES] = jnp.where(low, res[0], res[1]).astype(BF16)
        if emit_lse:
            lse_ref[:, m * LANES:(m + 1) * LANES] = jnp.where(low, lses[0], lses[1])


def _banded_attention(q, k, v, n_seq, seq, dil, radius, slot_of_pair, sink=None, emit_lse=False):
    n, qw = q.shape
    kw = k.shape[1]
    sub = seq // dil
    nb = sub // T_BAND
    view = lambda t: t.reshape(n // dil, dil * t.shape[1])
    cur = lambda b, r, i: (b * nb + i, r)
    prev = lambda b, r, i: (b * nb + jnp.maximum(i - 1, 0), r)
    nxt = lambda b, r, i: (b * nb + jnp.minimum(i + 1, nb - 1), r)
    kspec = lambda f: pl.BlockSpec((T_BAND, kw), f)
    in_specs = [pl.BlockSpec((T_BAND, qw), cur),
                kspec(prev), kspec(cur), kspec(nxt), kspec(prev), kspec(cur), kspec(nxt),
                pl.BlockSpec((T_BAND, 3 * T_BAND), lambda b, r, i: (0, 0))]
    args = [view(q), view(k), view(k), view(k), view(v), view(v), view(v), jnp.asarray(_band_bias(radius))]
    if sink is not None:
        in_specs = [pl.BlockSpec(memory_space=pltpu.SMEM)] + in_specs
        args = [sink] + args
    out_specs = [pl.BlockSpec((T_BAND, qw), cur)]
    out_shape = [jax.ShapeDtypeStruct((n // dil, dil * qw), BF16)]
    if emit_lse:
        out_specs.append(pl.BlockSpec((T_BAND, qw), cur))
        out_shape.append(jax.ShapeDtypeStruct((n // dil, dil * qw), F32))
    outs = pl.pallas_call(
        functools.partial(_banded_kernel, n_pairs=qw // LANES, slot_of_pair=slot_of_pair,
                          has_sink=sink is not None, emit_lse=emit_lse),
        grid=(n_seq, dil, nb),
        in_specs=in_specs, out_specs=out_specs, out_shape=out_shape,
        compiler_params=_params("parallel", "parallel", "parallel"),
        name="banded_attention",
    )(*args)
    return [o.reshape(n, qw) for o in outs]


def _outproj_even_kernel(oa_ref, o1_ref, o2_ref, o3_ref, l1_ref, l2_ref, l3_ref, wa_ref, wb_ref,
                         x_ref, g_ref, b_ref, y_ref):
    l1, l2, l3 = l1_ref[...], l2_ref[...], l3_ref[...]
    mx = jnp.maximum(jnp.maximum(l1, l2), l3)
    e1, e2, e3 = jnp.exp(l1 - mx), jnp.exp(l2 - mx), jnp.exp(l3 - mx)
    ob = (e1 * o1_ref[...].astype(F32) + e2 * o2_ref[...].astype(F32) + e3 * o3_ref[...].astype(F32)) / (e1 + e2 + e3)
    mix = (jnp.dot(oa_ref[...], wa_ref[...], preferred_element_type=F32)
           + jnp.dot(ob.astype(BF16), wb_ref[...], preferred_element_type=F32))
    y_ref[...] = _layer_norm(DN_ALPHA * x_ref[...] + mix, g_ref[...], b_ref[...])


def _outproj_even(oa, obs, lses, wa, wb, x, g, b):
    n = x.shape[0]
    row = lambda i: (i, 0)
    half = pl.BlockSpec((TM, 512), row)
    return pl.pallas_call(
        _outproj_even_kernel,
        grid=(n // TM,),
        in_specs=[half] * 7 + [_const_spec(wa.shape), _const_spec(wb.shape),
                               pl.BlockSpec((TM, D_MODEL), row), _const_spec(g.shape), _const_spec(b.shape)],
        out_specs=pl.BlockSpec((TM, D_MODEL), row),
        out_shape=jax.ShapeDtypeStruct((n, D_MODEL), F32),
        compiler_params=_params("parallel"),
        name="outproj_even",
    )(oa, *obs, *lses, wa, wb, x, g, b)


def _outproj_odd_kernel(o_ref, w_ref, x_ref, g_ref, b_ref, y_ref):
    mix = jnp.dot(o_ref[...], w_ref[...], preferred_element_type=F32)
    y_ref[...] = _layer_norm(DN_ALPHA * x_ref[...] + mix, g_ref[...], b_ref[...])


def _outproj_odd(o, w, x, g, b):
    n = x.shape[0]
    row = lambda i: (i, 0)
    return pl.pallas_call(
        _outproj_odd_kernel,
        grid=(n // TM,),
        in_specs=[pl.BlockSpec((TM, o.shape[1]), row), _const_spec(w.shape),
                  pl.BlockSpec((TM, D_MODEL), row), _const_spec(g.shape), _const_spec(b.shape)],
        out_specs=pl.BlockSpec((TM, D_MODEL), row),
        out_shape=jax.ShapeDtypeStruct((n, D_MODEL), F32),
        compiler_params=_params("parallel"),
        name="outproj_odd",
    )(o, w, x, g, b)


def _inproj_odd_kernel(x_ref, wq_ref, wk_ref, wv_ref, tb_ref, q_ref, k_ref, v_ref):
    xb = x_ref[...].astype(BF16)
    dot = functools.partial(jnp.dot, preferred_element_type=F32)
    q12 = dot(xb, wq_ref[...])
    q_ref[...] = (q12[:, :1024] * _tile_lanes(tb_ref[0], 8)
                  + q12[:, 1024:] * _tile_lanes(tb_ref[1], 8)).astype(BF16)
    k12 = dot(xb, wk_ref[...])
    k_ref[...] = (k12[:, :512] * _tile_lanes(tb_ref[2], 4)
                  + k12[:, 512:] * _tile_lanes(tb_ref[3], 4)).astype(BF16)
    v_ref[...] = dot(xb, wv_ref[...]).astype(BF16)


def _inproj_odd(x, w, tb, seq):
    n = x.shape[0]
    pos_blocks = seq // TM
    row = lambda i: (i, 0)
    ws = [w["wq"], w["wk"], w["wv"]]
    out_w = (1024, 512, 512)
    return pl.pallas_call(
        _inproj_odd_kernel,
        grid=(n // TM,),
        in_specs=[pl.BlockSpec((TM, D_MODEL), row)] + [_const_spec(a.shape) for a in ws]
        + [pl.BlockSpec((4, TM, LANES), lambda i: (0, i % pos_blocks, 0))],
        out_specs=[pl.BlockSpec((TM, c), row) for c in out_w],
        out_shape=[jax.ShapeDtypeStruct((n, c), BF16) for c in out_w],
        compiler_params=_params("parallel"),
        name="inproj_odd",
    )(x, *ws, tb)


def _extract_sorted(cur, k, out_ref, lanes):
    for i in range(k):
        mx = jnp.max(cur, axis=0, keepdims=True)
        out_ref[i:i + 1, lanes] = mx
        if i + 1 < k:
            cur = jnp.where(cur >= mx, NEG, cur)


def _kth_largest(cur, k):
    for i in range(k):
        mx = jnp.max(cur, axis=0, keepdims=True)
        if i + 1 < k:
            cur = jnp.where(cur >= mx, NEG, cur)
    return mx


def _split_bf16(t):
    hi = t.astype(BF16)
    return hi, (t - hi.astype(F32)).astype(BF16)


def _dot3(a_hi, a_lo, b_hi, b_lo, dims):
    d = functools.partial(lax.dot_general, dimension_numbers=dims, preferred_element_type=F32)
    return d(a_hi, b_hi) + (d(a_hi, b_lo) + d(a_lo, b_hi))


def _peer_select_kernel(x_ref, wqh_ref, wql_ref, kh_ref, kl_ref, e2_ref, thr_ref, e1_ref,
                        qh_ref, ql_ref, a_ref, b_ref):
    h = pl.program_id(1)

    @pl.when(h == 0)
    def _project():
        xh, xl = _split_bf16(x_ref[...])
        q = _dot3(wqh_ref[...], wql_ref[...], xh, xl, _NT)
        qh_ref[...], ql_ref[...] = _split_bf16(q)

    half = PEER_D_KEY // 2
    row0 = pl.multiple_of(h * PEER_D_KEY, PEER_D_KEY)
    mm = (((1,), (0,)), ((), ()))
    s1 = _dot3(kh_ref[0, 0], kl_ref[0, 0], qh_ref[pl.ds(row0, half), :], ql_ref[pl.ds(row0, half), :], mm)
    s2 = _dot3(kh_ref[0, 1], kl_ref[0, 1], qh_ref[pl.ds(row0 + half, half), :],
               ql_ref[pl.ds(row0 + half, half), :], mm)
    for lt in range(TM_SEL // LANES):
        ls = slice(lt * LANES, (lt + 1) * LANES)
        s1t, s2t = s1[:, ls], s2[:, ls]
        _extract_sorted(s1t, PEER_TOPK, a_ref, ls)
        _extract_sorted(s2t, PEER_TOPK, b_ref, ls)
        a16, b16 = a_ref[:, ls], b_ref[:, ls]
        a0, b0 = a16[0:1], b16[0:1]
        cand = [a0 + b16]
        cand += [a16[i:i + 1] + b16[0:8] for i in range(1, 8)]
        cand += [a16[8:16] + b0]
        cand = jnp.concatenate(cand, axis=0)
        v16 = _kth_largest(cand, PEER_TOPK)
        z = jnp.sum(jnp.where(cand >= v16, jnp.exp(cand - (a0 + b0)), 0.0), axis=0, keepdims=True)
        half_inv_z = 0.5 / z
        e2_sorted = jnp.exp(b16 - b0) * half_inv_z
        thr = jnp.full(s1t.shape, NO_KEY_QUALIFIES, F32)
        for i in range(PEER_TOPK):
            t_i = jnp.min(jnp.where(a16[i:i + 1] + b16 >= v16, e2_sorted, NO_KEY_QUALIFIES), axis=0, keepdims=True)
            thr = jnp.where(s1t == a16[i:i + 1], t_i, thr)
        e2_ref[:, ls] = jnp.exp(s2t - b0) * half_inv_z
        thr_ref[:, ls] = thr
        e1_ref[:, ls] = jnp.exp(s1t - a0)


def _peer_select(x, wq_hi, wq_lo, keys_hi, keys_lo):
    n = x.shape[0]
    spec = pl.BlockSpec((PEER_N_KEYS, TM_SEL), lambda i, h: (h, i))
    shape = lambda dt: jax.ShapeDtypeStruct((PEER_HEADS * PEER_N_KEYS, n), dt)
    kspec = pl.BlockSpec((1, 2, PEER_N_KEYS, PEER_D_KEY // 2), lambda i, h: (h, 0, 0, 0))
    return pl.pallas_call(
        _peer_select_kernel,
        grid=(n // TM_SEL, PEER_HEADS),
        in_specs=[pl.BlockSpec((TM_SEL, D_MODEL), lambda i, h: (i, 0)),
                  _const_spec(wq_hi.shape), _const_spec(wq_lo.shape), kspec, kspec],
        out_specs=[spec] * 3,
        out_shape=[shape(F32)] * 3,
        scratch_shapes=[pltpu.VMEM((PEER_HEADS * PEER_D_KEY, TM_SEL), BF16),
                        pltpu.VMEM((PEER_HEADS * PEER_D_KEY, TM_SEL), BF16),
                        pltpu.VMEM((PEER_TOPK, TM_SEL), F32),
                        pltpu.VMEM((PEER_TOPK, TM_SEL), F32)],
        compiler_params=_params("parallel", "arbitrary"),
        name="peer_select",
    )(x, wq_hi, wq_lo, keys_hi, keys_lo)


def _peer_expert_kernel(x_ref, e2_ref, thr_ref, e1_ref, u_ref, vt_ref, g_ref, b_ref, y_ref,
                        xb_ref, a_ref, h_ref, acc_ref):
    eb = pl.program_id(1)
    n_i1 = EB_PEER // PEER_N_KEYS
    halves = 2
    rows_half = EB_PEER // halves
    n_r = 2
    slab = PEER_N_KEYS // 2

    @pl.when(eb == 0)
    def _init():
        xb_ref[...] = x_ref[...].astype(BF16)
        acc_ref[...] = jnp.zeros(acc_ref.shape, F32)

    for hh in range(halves):
        rs = slice(hh * rows_half, (hh + 1) * rows_half)
        a_ref[rs, :] = lax.dot_general(u_ref[rs, :], xb_ref[...], _NT, preferred_element_type=F32)
    i1_base = pl.multiple_of(eb * n_i1, SUBLANES)
    for hh in range(halves):
        for lt in range(TM_PEER // LANES):
            ls = slice(lt * LANES, (lt + 1) * LANES)
            for rp in range(n_i1 // halves // n_r):
                r0 = hh * (n_i1 // halves) + n_r * rp
                for sb in range(PEER_N_KEYS // slab):
                    gate = [jnp.zeros((slab, LANES), F32)] * n_r
                    for h in range(PEER_HEADS):
                        t8 = thr_ref[pl.ds(h * PEER_N_KEYS + i1_base, n_i1), ls]
                        e8 = e1_ref[pl.ds(h * PEER_N_KEYS + i1_base, n_i1), ls]
                        e2t = e2_ref[h * PEER_N_KEYS + sb * slab:h * PEER_N_KEYS + (sb + 1) * slab, ls]
                        for t in range(n_r):
                            r = r0 + t
                            gate[t] = gate[t] + jnp.where(e2t >= t8[r:r + 1], e2t, 0.0) * e8[r:r + 1]
                    for t in range(n_r):
                        rows = slice((r0 + t) * PEER_N_KEYS + sb * slab, (r0 + t) * PEER_N_KEYS + (sb + 1) * slab)
                        a = a_ref[rows, ls]
                        h_ref[rows, ls] = (a * (1.0 + lax.erf(a * INV_SQRT2)) * gate[t]).astype(BF16)
        rs = slice(hh * rows_half, (hh + 1) * rows_half)
        acc_ref[...] += jnp.dot(vt_ref[:, rs], h_ref[rs, :], preferred_element_type=F32)

    @pl.when(eb == pl.num_programs(1) - 1)
    def _fin():
        y = DN_ALPHA * x_ref[...] + acc_ref[...].T
        y_ref[...] = _layer_norm(y, g_ref[...], b_ref[...])


def _peer_experts(x, stats, u_bf, vt_bf, g, b):
    n = x.shape[0]
    stat_spec = pl.BlockSpec((PEER_HEADS * PEER_N_KEYS, TM_PEER), lambda i, e: (0, i))
    return pl.pallas_call(
        _peer_expert_kernel,
        grid=(n // TM_PEER, PEER_N_EXPERTS // EB_PEER),
        in_specs=[pl.BlockSpec((TM_PEER, D_MODEL), lambda i, e: (i, 0))] + [stat_spec] * 3
        + [pl.BlockSpec((EB_PEER, D_MODEL), lambda i, e: (e, 0)),
           pl.BlockSpec((D_MODEL, EB_PEER), lambda i, e: (0, e)),
           _const_spec(g.shape), _const_spec(b.shape)],
        out_specs=pl.BlockSpec((TM_PEER, D_MODEL), lambda i, e: (i, 0)),
        out_shape=jax.ShapeDtypeStruct((n, D_MODEL), F32),
        scratch_shapes=[pltpu.VMEM((TM_PEER, D_MODEL), BF16),
                        pltpu.VMEM((EB_PEER, TM_PEER), F32),
                        pltpu.VMEM((EB_PEER, TM_PEER), BF16),
                        pltpu.VMEM((D_MODEL, TM_PEER), F32)],
        compiler_params=_params("parallel", "arbitrary"),
        name="peer_experts",
    )(x, *stats, u_bf, vt_bf, g, b)


def _ple_kernel(x_ref, p_ref, wp_ref, wg_ref, y_ref):
    x = x_ref[...]
    proj = jnp.dot(p_ref[...].astype(BF16), wp_ref[...], preferred_element_type=F32)
    gate = jax.nn.sigmoid(jnp.dot(x.astype(BF16), wg_ref[...], preferred_element_type=F32))
    y_ref[...] = x + proj * gate


def _ple(x, p, wp, wg):
    n = x.shape[0]
    row = lambda i: (i, 0)
    return pl.pallas_call(
        _ple_kernel,
        grid=(n // TM,),
        in_specs=[pl.BlockSpec((TM, D_MODEL), row), pl.BlockSpec((TM, PLE_DIM), row),
                  _const_spec(wp.shape), _const_spec(wg.shape)],
        out_specs=pl.BlockSpec((TM, D_MODEL), row),
        out_shape=jax.ShapeDtypeStruct((n, D_MODEL), F32),
        compiler_params=_params("parallel"),
        name="ple_gate",
    )(x, p, wp, wg)


def _rope_partner_cols(w, head_dim, half):
    n_heads = w.shape[1] // head_dim
    w3 = w.reshape(w.shape[0], n_heads, head_dim)
    sw = jnp.concatenate([w3[..., half:2 * half], w3[..., :half],
                          jnp.zeros_like(w3[..., 2 * half:])], axis=-1)
    return sw.reshape(w.shape)


def _twice_per_head(w, head_dim):
    n_heads = w.shape[1] // head_dim
    w3 = w.reshape(w.shape[0], n_heads, head_dim)
    return jnp.concatenate([w3, w3], axis=-1).reshape(w.shape[0], 2 * w.shape[1])


def _rope_tables(seq, half, theta, lanes_before, lanes_after, pass_through, scale):
    inv = theta ** (-jnp.arange(half, dtype=F32) / half)
    ang = jnp.arange(seq).astype(F32)[:, None] * inv[None, :]
    cos, sin = jnp.cos(ang), jnp.sin(ang)
    ones = lambda w, v: jnp.full((seq, w), v, F32)
    ct = jnp.concatenate([ones(lanes_before, pass_through), cos, cos, ones(lanes_after, pass_through)], axis=-1)
    st = jnp.concatenate([ones(lanes_before, 0.0), -sin, sin, ones(lanes_after, 0.0)], axis=-1)
    return ct * scale, st * scale


def _prep_even(a_w_in, a_q_norm, a_kv_norm, a_w_uq, a_w_ukv, a_w_out):
    o1 = MLA_Q_LORA
    o2 = o1 + MLA_KV_LORA
    o3 = o2 + MLA_ROPE
    nb = DIL_HEADS * DIL_HD
    bf = lambda t: t.astype(BF16)
    w = {}
    w["wcq"] = bf(a_w_in[:, :o1])
    w["wckv"] = bf(a_w_in[:, o1:o2])
    kr = a_w_in[:, o2:o3]
    krs = jnp.concatenate([kr[:, MLA_ROPE // 2:], kr[:, :MLA_ROPE // 2]], axis=-1)
    pad_slot = lambda t: jnp.pad(t, ((0, 0), (MLA_NOPE, MLA_SLOT - MLA_NOPE - MLA_ROPE)))
    w["wkr"] = bf(jnp.concatenate([pad_slot(kr), pad_slot(krs)], axis=-1))
    qb = a_w_in[:, o3:o3 + nb]
    kb = a_w_in[:, o3 + nb:o3 + 2 * nb]
    rd2 = DIL_HD // ROPE_FRACTION // 2
    w["wqb"] = bf(jnp.concatenate([qb, _rope_partner_cols(qb, DIL_HD, rd2)], axis=-1))
    w["wkb"] = bf(jnp.concatenate([kb, _rope_partner_cols(kb, DIL_HD, rd2)], axis=-1))
    w["wvb"] = bf(a_w_in[:, o3 + 2 * nb:])
    w["qn"] = a_q_norm.reshape(1, -1)
    w["kvn"] = a_kv_norm.reshape(1, -1)
    uq = a_w_uq.reshape(MLA_Q_LORA, MLA_HEADS, MLA_NOPE + MLA_ROPE)
    zpad = jnp.zeros((MLA_Q_LORA, MLA_HEADS, MLA_SLOT - MLA_NOPE - MLA_ROPE), F32)
    uq1 = jnp.concatenate([uq, zpad], axis=-1).reshape(MLA_Q_LORA, MLA_HEADS * MLA_SLOT)
    r = uq[..., MLA_NOPE:]
    uq2 = jnp.concatenate([jnp.zeros_like(uq[..., :MLA_NOPE]), r[..., MLA_ROPE // 2:], r[..., :MLA_ROPE // 2],
                           zpad], axis=-1).reshape(MLA_Q_LORA, MLA_HEADS * MLA_SLOT)
    w["wuq"] = bf(jnp.concatenate([uq1, uq2], axis=-1))
    ukv = a_w_ukv.reshape(MLA_KV_LORA, MLA_HEADS, MLA_NOPE + MLA_V)
    kpad = jnp.zeros((MLA_KV_LORA, MLA_HEADS, MLA_SLOT - MLA_NOPE), F32)
    w["wk"] = bf(jnp.concatenate([ukv[..., :MLA_NOPE], kpad], axis=-1).reshape(MLA_KV_LORA, -1))
    w["wv"] = bf(ukv[..., MLA_NOPE:].reshape(MLA_KV_LORA, MLA_HEADS * MLA_V))
    w["wout_a"] = bf(a_w_out[:MLA_HEADS * MLA_V])
    w["wout_b"] = bf(a_w_out[MLA_HEADS * MLA_V:])
    return w


def _prep_odd(c_w_in, c_w_out):
    e1 = C_HEADS * C_HD
    e2 = e1 + C_KV_HEADS * C_HD
    rd2 = C_HD // ROPE_FRACTION // 2
    q, k = c_w_in[:, :e1], c_w_in[:, e1:e2]
    k2 = lambda t: _twice_per_head(t, C_HD)
    return {
        "wq": jnp.concatenate([q, _rope_partner_cols(q, C_HD, rd2)], axis=-1).astype(BF16),
        "wk": jnp.concatenate([k2(k), k2(_rope_partner_cols(k, C_HD, rd2))], axis=-1).astype(BF16),
        "wv": k2(c_w_in[:, e2:]).astype(BF16),
        "wout": c_w_out.astype(BF16),
    }


def _split_kernel(w_ref, hi_ref, lo_ref):
    hi_ref[...], lo_ref[...] = _split_bf16(w_ref[...])


def _split_hi_lo(w):
    flat = w.reshape(-1, LANES)
    rows = flat.shape[0]
    blk = min(rows, 1024)
    spec = pl.BlockSpec((blk, LANES), lambda i: (i, 0))
    hi, lo = pl.pallas_call(
        _split_kernel,
        grid=(rows // blk,),
        in_specs=[spec], out_specs=[spec, spec],
        out_shape=[jax.ShapeDtypeStruct(flat.shape, BF16)] * 2,
        compiler_params=_params("parallel"),
        name="split_hi_lo",
    )(flat)
    return hi.reshape(w.shape), lo.reshape(w.shape)


def _prep_peer(peer_w_q, peer_sub_keys, peer_u, peer_v):
    wq_hi, wq_lo = _split_hi_lo(jnp.swapaxes(peer_w_q, 1, 2))
    keys_hi, keys_lo = _split_hi_lo(peer_sub_keys)
    return {
        "wq_hi": wq_hi, "wq_lo": wq_lo, "keys_hi": keys_hi, "keys_lo": keys_lo,
        "u": peer_u.astype(BF16), "vt": jnp.swapaxes(peer_v, 1, 2).astype(BF16),
    }


def _tables_mla(seq):
    scale = (MLA_NOPE + MLA_ROPE) ** -0.5
    pad = MLA_SLOT - MLA_NOPE - MLA_ROPE
    cq, sq = _rope_tables(seq, MLA_ROPE // 2, MLA_THETA, MLA_NOPE, pad, 1.0, scale)
    ck, sk = _rope_tables(seq, MLA_ROPE // 2, MLA_THETA, MLA_NOPE, pad, 0.0, 1.0)
    return jnp.stack([cq, sq, ck, sk])


def _tables_partial(seq, head_dim, q_scale):
    half = head_dim // ROPE_FRACTION // 2
    cq, sq = _rope_tables(seq, half, ROPE_THETA, 0, head_dim - 2 * half, 1.0, q_scale)
    ck, sk = _rope_tables(seq, half, ROPE_THETA, 0, head_dim - 2 * half, 1.0, 1.0)
    two = lambda t: jnp.concatenate([t, t], axis=-1)
    return jnp.stack([two(cq), two(sq), two(ck), two(sk)])


def _peer_layer(x, pw, i, g, b):
    stats = _peer_select(x, pw["wq_hi"][i], pw["wq_lo"][i], pw["keys_hi"][i], pw["keys_lo"][i])
    return _peer_experts(x, stats, pw["u"][i], pw["vt"][i], g, b)


def _mixer_even(x, w, n_seq, seq, g, b):
    qa, ka, va, qb, kb, vb = _inproj_even(x, w, _tables_mla(seq), _tables_partial(seq, DIL_HD, DIL_HD ** -0.5), seq)
    oa = _mla_attention(qa, ka, va, n_seq, seq)
    obs, lses = [], []
    for window, dil in DIL_PATTERNS:
        o, lse = _banded_attention(qb, kb, vb, n_seq, seq, dil, (window // 2) // dil, lambda m: m, emit_lse=True)
        obs.append(o)
        lses.append(lse)
    return _outproj_even(oa, obs, lses, w["wout_a"], w["wout_b"], x, g, b)


def _mixer_odd(x, w, sink, n_seq, seq, g, b):
    q, k, v = _inproj_odd(x, w, _tables_partial(seq, C_HD, C_HD ** -0.5), seq)
    pairs_per_group = C_HEADS // C_KV_HEADS // 2
    (o,) = _banded_attention(q, k, v, n_seq, seq, 1, C_RADIUS, lambda m: m // pairs_per_group, sink=sink)
    return _outproj_odd(o, w["wout"], x, g, b)


def _trunk(x3, p4, prm):
    n_seq, seq, _ = x3.shape
    x = x3.reshape(n_seq * seq, D_MODEL)
    p = p4.reshape(DEPTH, n_seq * seq, PLE_DIM)
    row = lambda t: t.reshape(1, -1)
    for i in range(DEPTH):
        g, b = row(prm["ln_mix_g"][i]), row(prm["ln_mix_b"][i])
        if i % 2 == 0:
            x = _mixer_even(x, prm["even"], n_seq, seq, g, b)
        else:
            x = _mixer_odd(x, prm["odd"], prm["sink"], n_seq, seq, g, b)
        x = _peer_layer(x, prm["peer"], i, row(prm["ln_ffn_g"][i]), row(prm["ln_ffn_b"][i]))
        x = _ple(x, p[i], prm["ple_proj"][i], prm["ple_gate"][i])
    return x.reshape(n_seq, seq, D_MODEL)


def kernel(x_prompt, x_sample, p_prompt, p_sample, a_w_in, a_q_norm, a_kv_norm, a_w_uq, a_w_ukv, a_w_out,
           c_w_in, c_sink, c_w_out, ln_mix_g, ln_mix_b, ln_ffn_g, ln_ffn_b, peer_w_q, peer_sub_keys,
           peer_u, peer_v, ple_proj, ple_gate):
    prm = {
        "even": _prep_even(a_w_in[0], a_q_norm[0], a_kv_norm[0], a_w_uq[0], a_w_ukv[0], a_w_out[0]),
        "odd": _prep_odd(c_w_in[0], c_w_out[0]),
        "sink": c_sink[0],
        "ln_mix_g": ln_mix_g, "ln_mix_b": ln_mix_b, "ln_ffn_g": ln_ffn_g, "ln_ffn_b": ln_ffn_b,
        "peer": _prep_peer(peer_w_q, peer_sub_keys, peer_u, peer_v),
        "ple_proj": ple_proj.astype(BF16),
        "ple_gate": ple_gate.astype(BF16),
    }
    y_prompt = _trunk(x_prompt, p_prompt, prm)
    y_sample = _trunk(x_sample, p_sample, prm)
    return (y_prompt, y_sample)
```

```python
import functools

import jax
import jax.numpy as jnp
import numpy as np
from jax import lax
from jax.experimental import pallas as pl
from jax.experimental.pallas import tpu as pltpu

F32 = jnp.float32
BF16 = jnp.bfloat16

D_MODEL = 1024
DEPTH = 2
PLE_DIM = 256
MLA_HEADS = 8
MLA_Q_LORA = 384
MLA_KV_LORA = 256
MLA_NOPE = 64
MLA_ROPE = 32
MLA_V = 64
MLA_THETA = 10000.0
MLA_SLOT = 128
DIL_HEADS = 8
DIL_HD = 64
DIL_PATTERNS = ((128, 1), (512, 4), (2048, 16))
DIL_W = DIL_HEADS * DIL_HD
C_HEADS = 16
C_KV_HEADS = 4
C_HD = 64
C_RADIUS = 128
ROPE_THETA = 500000.0
ROPE_FRACTION = 4
PEER_HEADS = 8
PEER_N_KEYS = 128
PEER_N_EXPERTS = PEER_N_KEYS * PEER_N_KEYS
PEER_D_KEY = 128
PEER_TOPK = 16
DN_ALPHA = (2 * DEPTH) ** 0.25
LN_EPS = 1e-5
RMS_EPS = 1e-6
NEG = -1e30
INV_SQRT2 = 0.7071067811865476
NO_KEY_QUALIFIES = 1.0

LANES = 128
SUBLANES = 8
VMEM_LIMIT_BYTES = 56 * 1024 * 1024

TM = 512
TQ_MLA = 512
TK_MLA = 1024
TQ_BAND = 256
T_HALO = 128
TM_SEL = 512
TM_PEER = 512
EB_PEER = 1024

_NT = (((1,), (1,)), ((), ()))


def _params(*sem):
    return pltpu.CompilerParams(dimension_semantics=sem, vmem_limit_bytes=VMEM_LIMIT_BYTES)


def _layer_norm(y, g, b):
    mu = jnp.mean(y, axis=-1, keepdims=True)
    d = y - mu
    var = jnp.mean(d * d, axis=-1, keepdims=True)
    return d * lax.rsqrt(var + LN_EPS) * g + b


def _rms_norm(y, g):
    return y * lax.rsqrt(jnp.mean(y * y, axis=-1, keepdims=True) + RMS_EPS) * g


def _tile_lanes(t, reps):
    return jnp.concatenate([t] * reps, axis=-1)


def _low_half_lanes():
    return lax.broadcasted_iota(jnp.int32, (1, LANES), 1) < LANES // 2


def _const_spec(shape):
    nd = len(shape)
    return pl.BlockSpec(shape, lambda *_: (0,) * nd)


def _inproj_even_kernel(x_ref, wcq_ref, wckv_ref, wkr_ref, wqb_ref, wkb_ref, wvb_ref,
                        qn_ref, kvn_ref, wuq_ref, wk_ref, wv_ref, ta_ref, tb_ref,
                        qa_ref, ka_ref, va_ref, qb_ref, kb_ref, vb_ref,
                        q4_ref, k4_ref, v4_ref, q16_ref, k16_ref, v16_ref, perm_ref):
    xb = x_ref[...].astype(BF16)
    dot = functools.partial(jnp.dot, preferred_element_type=F32)
    cq = _rms_norm(dot(xb, wcq_ref[...]), qn_ref[...]).astype(BF16)
    q12 = dot(cq, wuq_ref[...])
    cq_t = _tile_lanes(ta_ref[0], MLA_HEADS)
    sq_t = _tile_lanes(ta_ref[1], MLA_HEADS)
    qa_ref[...] = (q12[:, :1024] * cq_t + q12[:, 1024:] * sq_t).astype(BF16)
    ckv = _rms_norm(dot(xb, wckv_ref[...]), kvn_ref[...]).astype(BF16)
    kr12 = dot(xb, wkr_ref[...])
    kr = kr12[:, :LANES] * ta_ref[2] + kr12[:, LANES:] * ta_ref[3]
    ka_ref[...] = (dot(ckv, wk_ref[...]) + _tile_lanes(kr, MLA_HEADS)).astype(BF16)
    va_ref[...] = dot(ckv, wv_ref[...]).astype(BF16)
    def emit(val, nat_ref, views):
        nat_ref[...] = val.astype(BF16)
        for lt in range(DIL_W // LANES):
            perm_ref[lt] = val[:, lt * LANES:(lt + 1) * LANES]
        for d, ref in views:
            for r in range(d):
                for lt in range(DIL_W // LANES):
                    c0 = r * DIL_W + lt * LANES
                    ref[:, c0:c0 + LANES] = perm_ref[lt, pl.ds(r, TM // d, stride=d), :].astype(BF16)

    q12b = dot(xb, wqb_ref[...])
    emit(q12b[:, :512] * _tile_lanes(tb_ref[0], 4) + q12b[:, 512:] * _tile_lanes(tb_ref[1], 4),
         qb_ref, ((4, q4_ref), (16, q16_ref)))
    k12b = dot(xb, wkb_ref[...])
    emit(k12b[:, :512] * _tile_lanes(tb_ref[2], 4) + k12b[:, 512:] * _tile_lanes(tb_ref[3], 4),
         kb_ref, ((4, k4_ref), (16, k16_ref)))
    emit(dot(xb, wvb_ref[...]), vb_ref, ((4, v4_ref), (16, v16_ref)))


def _inproj_even(x, w, ta, tb, seq):
    n = x.shape[0]
    pos_blocks = seq // TM
    row = lambda i: (i, 0)
    tab = lambda i: (0, i % pos_blocks, 0)
    wnames = ("wcq", "wckv", "wkr", "wqb", "wkb", "wvb", "qn", "kvn", "wuq", "wk", "wv")
    ws = [w[k] for k in wnames]
    out_w = (1024, 1024, 512, 512, 512, 512)
    views = [4, 4, 4, 16, 16, 16]
    return pl.pallas_call(
        _inproj_even_kernel,
        grid=(n // TM,),
        in_specs=[pl.BlockSpec((TM, D_MODEL), row)] + [_const_spec(a.shape) for a in ws]
        + [pl.BlockSpec((4, TM, LANES), tab), pl.BlockSpec((4, TM, LANES), tab)],
        out_specs=[pl.BlockSpec((TM, c), row) for c in out_w]
        + [pl.BlockSpec((TM // d, d * DIL_W), row) for d in views],
        out_shape=[jax.ShapeDtypeStruct((n, c), BF16) for c in out_w]
        + [jax.ShapeDtypeStruct((n // d, d * DIL_W), BF16) for d in views],
        scratch_shapes=[pltpu.VMEM((DIL_W // LANES, TM, LANES), F32)],
        compiler_params=_params("parallel"),
        name="inproj_even",
    )(x, *ws, ta, tb)


def _mla_kernel(q_ref, k_ref, v_ref, o_ref, m_ref, l_ref, acc_ref):
    ki = pl.program_id(2)
    low = _low_half_lanes()
    reps = TK_MLA // LANES

    @pl.when(ki == 0)
    def _init():
        m_ref[...] = jnp.full(m_ref.shape, NEG, F32)
        l_ref[...] = jnp.zeros(l_ref.shape, F32)
        acc_ref[...] = jnp.zeros(acc_ref.shape, F32)

    for j in range(MLA_HEADS // 2):
        vslot = v_ref[:, j * LANES:(j + 1) * LANES]
        prod, alphas = [], []
        for half in range(2):
            h = 2 * j + half
            qh = q_ref[:, h * MLA_SLOT:(h + 1) * MLA_SLOT]
            kh = k_ref[:, h * MLA_SLOT:(h + 1) * MLA_SLOT]
            s = lax.dot_general(qh, kh, _NT, preferred_element_type=F32)
            m_prev = m_ref[h]
            m_new = jnp.maximum(m_prev, jnp.max(s, axis=-1, keepdims=True))
            alpha = jnp.exp(m_prev - m_new)
            p = jnp.exp(s - _tile_lanes(m_new, reps))
            l_ref[h] = alpha * l_ref[h] + jnp.sum(p, axis=-1, keepdims=True)
            m_ref[h] = m_new
            prod.append(jnp.dot(p.astype(BF16), vslot, preferred_element_type=F32))
            alphas.append(alpha)
        acc_ref[j] = (jnp.where(low, alphas[0], alphas[1]) * acc_ref[j]
                      + jnp.where(low, prod[0], prod[1]))

    @pl.when(ki == pl.num_programs(2) - 1)
    def _fin():
        outs = [acc_ref[j] / jnp.where(low, l_ref[2 * j], l_ref[2 * j + 1]) for j in range(MLA_HEADS // 2)]
        o_ref[...] = jnp.concatenate(outs, axis=-1).astype(BF16)


def _mla_attention(q, k, v, n_seq, seq):
    n = q.shape[0]
    nq, nk = seq // TQ_MLA, seq // TK_MLA
    return pl.pallas_call(
        _mla_kernel,
        grid=(n_seq, nq, nk),
        in_specs=[pl.BlockSpec((TQ_MLA, 1024), lambda b, i, j: (b * nq + i, 0)),
                  pl.BlockSpec((TK_MLA, 1024), lambda b, i, j: (b * nk + j, 0)),
                  pl.BlockSpec((TK_MLA, 512), lambda b, i, j: (b * nk + j, 0))],
        out_specs=pl.BlockSpec((TQ_MLA, 512), lambda b, i, j: (b * nq + i, 0)),
        out_shape=jax.ShapeDtypeStruct((n, 512), BF16),
        scratch_shapes=[pltpu.VMEM((MLA_HEADS, TQ_MLA, LANES), F32),
                        pltpu.VMEM((MLA_HEADS, TQ_MLA, LANES), F32),
                        pltpu.VMEM((MLA_HEADS // 2, TQ_MLA, LANES), F32)],
        compiler_params=_params("parallel", "parallel", "arbitrary"),
        name="mla_attention",
    )(q, k, v)


def _band_bias(radius, tq):
    row = np.arange(tq)[:, None]
    col = np.arange(tq + 2 * T_HALO)[None, :]
    d = col - T_HALO - row
    return np.where(np.abs(d) <= radius, 0.0, NEG).astype(np.float32)


def _banded_kernel(*refs, tq, n_pairs, slot_of_pair, has_sink, emit_lse):
    refs = list(refs)
    sink_ref = refs.pop(0) if has_sink else None
    q_ref, kp_ref, kc_ref, kn_ref, vp_ref, vc_ref, vn_ref, bias_ref = refs[:8]
    o_ref = refs[8]
    lse_ref = refs[9] if emit_lse else None
    qi = pl.program_id(2)
    nb = pl.num_programs(2)
    col = lax.broadcasted_iota(jnp.int32, (1, tq + 2 * T_HALO), 1)
    no_prev = jnp.where(qi == 0, NEG, 0.0)
    no_next = jnp.where(qi == nb - 1, NEG, 0.0)
    bias = (bias_ref[...] + jnp.where(col < T_HALO, no_prev, 0.0)
            + jnp.where(col >= tq + T_HALO, no_next, 0.0))
    low = _low_half_lanes()
    for m in range(n_pairs):
        sl = slice(slot_of_pair(m) * LANES, (slot_of_pair(m) + 1) * LANES)
        kcat = jnp.concatenate([kp_ref[:, sl], kc_ref[:, sl], kn_ref[:, sl]], axis=0)
        vcat = jnp.concatenate([vp_ref[:, sl], vc_ref[:, sl], vn_ref[:, sl]], axis=0)
        qp = q_ref[:, m * LANES:(m + 1) * LANES]
        res, lses = [], []
        for half in range(2):
            keep = low if half == 0 else jnp.logical_not(low)
            qm = jnp.where(keep, qp, jnp.zeros_like(qp))
            s = lax.dot_general(qm, kcat, _NT, preferred_element_type=F32) + bias
            mx = jnp.max(s, axis=-1, keepdims=True)
            if has_sink:
                sk = sink_ref[2 * m + half]
                mx = jnp.maximum(mx, sk)
            p = jnp.exp(s - mx)
            den = jnp.sum(p, axis=-1, keepdims=True)
            if has_sink:
                den = den + jnp.exp(sk - mx)
            res.append(jnp.dot(p.astype(BF16), vcat, preferred_element_type=F32) / den)
            if emit_lse:
                lses.append(mx + jnp.log(den))
        o_ref[:, m * LANES:(m + 1) * LANES] = jnp.where(low, res[0], res[1]).astype(BF16)
        if emit_lse:
            lse_ref[:, m * LANES:(m + 1) * LANES] = jnp.where(low, lses[0], lses[1])


def _banded_attention(q, k, v, n_seq, seq, dil, radius, slot_of_pair, sink=None, emit_lse=False):
    n, qw = q.shape[0] * dil, q.shape[1] // dil
    kw = k.shape[1] // dil
    sub = seq // dil
    tq = min(TQ_BAND, sub)
    nb = sub // tq
    hq = tq // T_HALO
    nh = sub // T_HALO
    cur = lambda b, r, i: (b * nb + i, r)
    prev = lambda b, r, i: (b * nh + jnp.maximum(i * hq - 1, 0), r)
    nxt = lambda b, r, i: (b * nh + jnp.minimum((i + 1) * hq, nh - 1), r)
    halo = lambda f: pl.BlockSpec((T_HALO, kw), f)
    body = pl.BlockSpec((tq, kw), cur)
    in_specs = [pl.BlockSpec((tq, qw), cur),
                halo(prev), body, halo(nxt), halo(prev), body, halo(nxt),
                pl.BlockSpec((tq, tq + 2 * T_HALO), lambda b, r, i: (0, 0))]
    args = [q, k, k, k, v, v, v, jnp.asarray(_band_bias(radius, tq))]
    if sink is not None:
        in_specs = [pl.BlockSpec(memory_space=pltpu.SMEM)] + in_specs
        args = [sink] + args
    out_specs = [pl.BlockSpec((tq, qw), cur)]
    out_shape = [jax.ShapeDtypeStruct((n // dil, dil * qw), BF16)]
    if emit_lse:
        out_specs.append(pl.BlockSpec((tq, qw), cur))
        out_shape.append(jax.ShapeDtypeStruct((n // dil, dil * qw), F32))
    return pl.pallas_call(
        functools.partial(_banded_kernel, tq=tq, n_pairs=qw // LANES, slot_of_pair=slot_of_pair,
                          has_sink=sink is not None, emit_lse=emit_lse),
        grid=(n_seq, dil, nb),
        in_specs=in_specs, out_specs=out_specs, out_shape=out_shape,
        compiler_params=_params("parallel", "parallel", "parallel"),
        name="banded_attention",
    )(*args)


def _outproj_even_kernel(oa_ref, o1_ref, l1_ref, o4_ref, l4_ref, o16_ref, l16_ref, wa_ref, wb_ref,
                         x_ref, g_ref, b_ref, y_ref, po4_ref, pl4_ref, po16_ref, pl16_ref):
    def natural(src_ref, dst_ref, d):
        for r in range(d):
            for lt in range(DIL_W // LANES):
                c0 = r * DIL_W + lt * LANES
                dst_ref[lt, pl.ds(r, TM // d, stride=d), :] = src_ref[:, c0:c0 + LANES].astype(F32)
        return jnp.concatenate([dst_ref[lt] for lt in range(DIL_W // LANES)], axis=-1)

    l1, o1 = l1_ref[...], o1_ref[...].astype(F32)
    l2, o2 = natural(l4_ref, pl4_ref, 4), natural(o4_ref, po4_ref, 4)
    l3, o3 = natural(l16_ref, pl16_ref, 16), natural(o16_ref, po16_ref, 16)
    mx = jnp.maximum(jnp.maximum(l1, l2), l3)
    e1, e2, e3 = jnp.exp(l1 - mx), jnp.exp(l2 - mx), jnp.exp(l3 - mx)
    ob = (e1 * o1 + e2 * o2 + e3 * o3) / (e1 + e2 + e3)
    mix = (jnp.dot(oa_ref[...], wa_ref[...], preferred_element_type=F32)
           + jnp.dot(ob.astype(BF16), wb_ref[...], preferred_element_type=F32))
    y_ref[...] = _layer_norm(DN_ALPHA * x_ref[...] + mix, g_ref[...], b_ref[...])


def _outproj_even(oa, pats, wa, wb, x, g, b):
    n = x.shape[0]
    row = lambda i: (i, 0)
    pat_specs, pat_args = [], []
    for (_, d), (o, lse) in zip(DIL_PATTERNS, pats):
        pat_specs += [pl.BlockSpec((TM // d, d * DIL_W), row)] * 2
        pat_args += [o, lse]
    return pl.pallas_call(
        _outproj_even_kernel,
        grid=(n // TM,),
        in_specs=[pl.BlockSpec((TM, DIL_W), row)] + pat_specs
        + [_const_spec(wa.shape), _const_spec(wb.shape),
           pl.BlockSpec((TM, D_MODEL), row), _const_spec(g.shape), _const_spec(b.shape)],
        out_specs=pl.BlockSpec((TM, D_MODEL), row),
        out_shape=jax.ShapeDtypeStruct((n, D_MODEL), F32),
        scratch_shapes=[pltpu.VMEM((DIL_W // LANES, TM, LANES), F32)] * 4,
        compiler_params=_params("parallel"),
        name="outproj_even",
    )(oa, *pat_args, wa, wb, x, g, b)


def _outproj_odd_kernel(o_ref, w_ref, x_ref, g_ref, b_ref, y_ref):
    mix = jnp.dot(o_ref[...], w_ref[...], preferred_element_type=F32)
    y_ref[...] = _layer_norm(DN_ALPHA * x_ref[...] + mix, g_ref[...], b_ref[...])


def _outproj_odd(o, w, x, g, b):
    n = x.shape[0]
    row = lambda i: (i, 0)
    return pl.pallas_call(
        _outproj_odd_kernel,
        grid=(n // TM,),
        in_specs=[pl.BlockSpec((TM, o.shape[1]), row), _const_spec(w.shape),
                  pl.BlockSpec((TM, D_MODEL), row), _const_spec(g.shape), _const_spec(b.shape)],
        out_specs=pl.BlockSpec((TM, D_MODEL), row),
        out_shape=jax.ShapeDtypeStruct((n, D_MODEL), F32),
        compiler_params=_params("parallel"),
        name="outproj_odd",
    )(o, w, x, g, b)


def _inproj_odd_kernel(x_ref, wq_ref, wk_ref, wv_ref, tb_ref, q_ref, k_ref, v_ref):
    xb = x_ref[...].astype(BF16)
    dot = functools.partial(jnp.dot, preferred_element_type=F32)
    q12 = dot(xb, wq_ref[...])
    q_ref[...] = (q12[:, :1024] * _tile_lanes(tb_ref[0], 8)
                  + q12[:, 1024:] * _tile_lanes(tb_ref[1], 8)).astype(BF16)
    k12 = dot(xb, wk_ref[...])
    k_ref[...] = (k12[:, :512] * _tile_lanes(tb_ref[2], 4)
                  + k12[:, 512:] * _tile_lanes(tb_ref[3], 4)).astype(BF16)
    v_ref[...] = dot(xb, wv_ref[...]).astype(BF16)


def _inproj_odd(x, w, tb, seq):
    n = x.shape[0]
    pos_blocks = seq // TM
    row = lambda i: (i, 0)
    ws = [w["wq"], w["wk"], w["wv"]]
    out_w = (1024, 512, 512)
    return pl.pallas_call(
        _inproj_odd_kernel,
        grid=(n // TM,),
        in_specs=[pl.BlockSpec((TM, D_MODEL), row)] + [_const_spec(a.shape) for a in ws]
        + [pl.BlockSpec((4, TM, LANES), lambda i: (0, i % pos_blocks, 0))],
        out_specs=[pl.BlockSpec((TM, c), row) for c in out_w],
        out_shape=[jax.ShapeDtypeStruct((n, c), BF16) for c in out_w],
        compiler_params=_params("parallel"),
        name="inproj_odd",
    )(x, *ws, tb)


def _extract_sorted(cur, k, out_ref, lanes):
    for i in range(k):
        mx = jnp.max(cur, axis=0, keepdims=True)
        out_ref[i:i + 1, lanes] = mx
        if i + 1 < k:
            cur = jnp.where(cur >= mx, NEG, cur)


def _kth_largest(cur, k):
    for i in range(k):
        mx = jnp.max(cur, axis=0, keepdims=True)
        if i + 1 < k:
            cur = jnp.where(cur >= mx, NEG, cur)
    return mx


def _split_bf16(t):
    hi = t.astype(BF16)
    return hi, (t - hi.astype(F32)).astype(BF16)


def _dot3(a_hi, a_lo, b_hi, b_lo, dims):
    d = functools.partial(lax.dot_general, dimension_numbers=dims, preferred_element_type=F32)
    return d(a_hi, b_hi) + (d(a_hi, b_lo) + d(a_lo, b_hi))


def _peer_select_kernel(x_ref, wqh_ref, wql_ref, kh_ref, kl_ref, e2_ref, thr_ref, e1_ref,
                        qh_ref, ql_ref, a_ref, b_ref):
    h = pl.program_id(1)

    @pl.when(h == 0)
    def _project():
        xh, xl = _split_bf16(x_ref[...])
        q = _dot3(wqh_ref[...], wql_ref[...], xh, xl, _NT)
        qh_ref[...], ql_ref[...] = _split_bf16(q)

    half = PEER_D_KEY // 2
    row0 = pl.multiple_of(h * PEER_D_KEY, PEER_D_KEY)
    mm = (((1,), (0,)), ((), ()))
    s1 = _dot3(kh_ref[0, 0], kl_ref[0, 0], qh_ref[pl.ds(row0, half), :], ql_ref[pl.ds(row0, half), :], mm)
    s2 = _dot3(kh_ref[0, 1], kl_ref[0, 1], qh_ref[pl.ds(row0 + half, half), :],
               ql_ref[pl.ds(row0 + half, half), :], mm)
    for lt in range(TM_SEL // LANES):
        ls = slice(lt * LANES, (lt + 1) * LANES)
        s1t, s2t = s1[:, ls], s2[:, ls]
        _extract_sorted(s1t, PEER_TOPK, a_ref, ls)
        _extract_sorted(s2t, PEER_TOPK, b_ref, ls)
        a16, b16 = a_ref[:, ls], b_ref[:, ls]
        a0, b0 = a16[0:1], b16[0:1]
        cand = [a0 + b16]
        cand += [a16[i:i + 1] + b16[0:8] for i in range(1, 8)]
        cand += [a16[8:16] + b0]
        cand = jnp.concatenate(cand, axis=0)
        v16 = _kth_largest(cand, PEER_TOPK)
        z = jnp.sum(jnp.where(cand >= v16, jnp.exp(cand - (a0 + b0)), 0.0), axis=0, keepdims=True)
        half_inv_z = 0.5 / z
        e2_sorted = jnp.exp(b16 - b0) * half_inv_z
        thr = jnp.full(s1t.shape, NO_KEY_QUALIFIES, F32)
        for i in range(PEER_TOPK):
            t_i = jnp.min(jnp.where(a16[i:i + 1] + b16 >= v16, e2_sorted, NO_KEY_QUALIFIES), axis=0, keepdims=True)
            thr = jnp.where(s1t == a16[i:i + 1], t_i, thr)
        e2_ref[:, ls] = jnp.exp(s2t - b0) * half_inv_z
        thr_ref[:, ls] = thr
        e1_ref[:, ls] = jnp.exp(s1t - a0)


def _peer_select(x, wq_hi, wq_lo, keys_hi, keys_lo):
    n = x.shape[0]
    spec = pl.BlockSpec((PEER_N_KEYS, TM_SEL), lambda i, h: (h, i))
    shape = lambda dt: jax.ShapeDtypeStruct((PEER_HEADS * PEER_N_KEYS, n), dt)
    kspec = pl.BlockSpec((1, 2, PEER_N_KEYS, PEER_D_KEY // 2), lambda i, h: (h, 0, 0, 0))
    return pl.pallas_call(
        _peer_select_kernel,
        grid=(n // TM_SEL, PEER_HEADS),
        in_specs=[pl.BlockSpec((TM_SEL, D_MODEL), lambda i, h: (i, 0)),
                  _const_spec(wq_hi.shape), _const_spec(wq_lo.shape), kspec, kspec],
        out_specs=[spec] * 3,
        out_shape=[shape(F32)] * 3,
        scratch_shapes=[pltpu.VMEM((PEER_HEADS * PEER_D_KEY, TM_SEL), BF16),
                        pltpu.VMEM((PEER_HEADS * PEER_D_KEY, TM_SEL), BF16),
                        pltpu.VMEM((PEER_TOPK, TM_SEL), F32),
                        pltpu.VMEM((PEER_TOPK, TM_SEL), F32)],
        compiler_params=_params("parallel", "arbitrary"),
        name="peer_select",
    )(x, wq_hi, wq_lo, keys_hi, keys_lo)


def _peer_expert_kernel(x_ref, e2_ref, thr_ref, e1_ref, u_ref, vt_ref, g_ref, b_ref, y_ref,
                        xb_ref, a_ref, h_ref, acc_ref):
    eb = pl.program_id(1)
    n_i1 = EB_PEER // PEER_N_KEYS
    halves = 2
    rows_half = EB_PEER // halves
    n_r = 2
    slab = PEER_N_KEYS // 2

    @pl.when(eb == 0)
    def _init():
        xb_ref[...] = x_ref[...].astype(BF16)
        acc_ref[...] = jnp.zeros(acc_ref.shape, F32)

    for hh in range(halves):
        rs = slice(hh * rows_half, (hh + 1) * rows_half)
        a_ref[rs, :] = lax.dot_general(u_ref[rs, :], xb_ref[...], _NT, preferred_element_type=F32)
    i1_base = pl.multiple_of(eb * n_i1, SUBLANES)
    for hh in range(halves):
        for lt in range(TM_PEER // LANES):
            ls = slice(lt * LANES, (lt + 1) * LANES)
            for rp in range(n_i1 // halves // n_r):
                r0 = hh * (n_i1 // halves) + n_r * rp
                for sb in range(PEER_N_KEYS // slab):
                    gate = [jnp.zeros((slab, LANES), F32)] * n_r
                    for h in range(PEER_HEADS):
                        t8 = thr_ref[pl.ds(h * PEER_N_KEYS + i1_base, n_i1), ls]
                        e8 = e1_ref[pl.ds(h * PEER_N_KEYS + i1_base, n_i1), ls]
                        e2t = e2_ref[h * PEER_N_KEYS + sb * slab:h * PEER_N_KEYS + (sb + 1) * slab, ls]
                        for t in range(n_r):
                            r = r0 + t
                            gate[t] = gate[t] + jnp.where(e2t >= t8[r:r + 1], e2t, 0.0) * e8[r:r + 1]
                    for t in range(n_r):
                        rows = slice((r0 + t) * PEER_N_KEYS + sb * slab, (r0 + t) * PEER_N_KEYS + (sb + 1) * slab)
                        a = a_ref[rows, ls]
                        h_ref[rows, ls] = (a * (1.0 + lax.erf(a * INV_SQRT2)) * gate[t]).astype(BF16)
        rs = slice(hh * rows_half, (hh + 1) * rows_half)
        acc_ref[...] += jnp.dot(vt_ref[:, rs], h_ref[rs, :], preferred_element_type=F32)

    @pl.when(eb == pl.num_programs(1) - 1)
    def _fin():
        y = DN_ALPHA * x_ref[...] + acc_ref[...].T
        y_ref[...] = _layer_norm(y, g_ref[...], b_ref[...])


def _peer_experts(x, stats, u_bf, vt_bf, g, b):
    n = x.shape[0]
    stat_spec = pl.BlockSpec((PEER_HEADS * PEER_N_KEYS, TM_PEER), lambda i, e: (0, i))
    return pl.pallas_call(
        _peer_expert_kernel,
        grid=(n // TM_PEER, PEER_N_EXPERTS // EB_PEER),
        in_specs=[pl.BlockSpec((TM_PEER, D_MODEL), lambda i, e: (i, 0))] + [stat_spec] * 3
        + [pl.BlockSpec((EB_PEER, D_MODEL), lambda i, e: (e, 0)),
           pl.BlockSpec((D_MODEL, EB_PEER), lambda i, e: (0, e)),
           _const_spec(g.shape), _const_spec(b.shape)],
        out_specs=pl.BlockSpec((TM_PEER, D_MODEL), lambda i, e: (i, 0)),
        out_shape=jax.ShapeDtypeStruct((n, D_MODEL), F32),
        scratch_shapes=[pltpu.VMEM((TM_PEER, D_MODEL), BF16),
                        pltpu.VMEM((EB_PEER, TM_PEER), F32),
                        pltpu.VMEM((EB_PEER, TM_PEER), BF16),
                        pltpu.VMEM((D_MODEL, TM_PEER), F32)],
        compiler_params=_params("parallel", "arbitrary"),
        name="peer_experts",
    )(x, *stats, u_bf, vt_bf, g, b)


def _ple_kernel(x_ref, p_ref, wp_ref, wg_ref, y_ref):
    x = x_ref[...]
    proj = jnp.dot(p_ref[...].astype(BF16), wp_ref[...], preferred_element_type=F32)
    gate = jax.nn.sigmoid(jnp.dot(x.astype(BF16), wg_ref[...], preferred_element_type=F32))
    y_ref[...] = x + proj * gate


def _ple(x, p, wp, wg):
    n = x.shape[0]
    row = lambda i: (i, 0)
    return pl.pallas_call(
        _ple_kernel,
        grid=(n // TM,),
        in_specs=[pl.BlockSpec((TM, D_MODEL), row), pl.BlockSpec((TM, PLE_DIM), row),
                  _const_spec(wp.shape), _const_spec(wg.shape)],
        out_specs=pl.BlockSpec((TM, D_MODEL), row),
        out_shape=jax.ShapeDtypeStruct((n, D_MODEL), F32),
        compiler_params=_params("parallel"),
        name="ple_gate",
    )(x, p, wp, wg)


def _rope_partner_cols(w, head_dim, half):
    n_heads = w.shape[1] // head_dim
    w3 = w.reshape(w.shape[0], n_heads, head_dim)
    sw = jnp.concatenate([w3[..., half:2 * half], w3[..., :half],
                          jnp.zeros_like(w3[..., 2 * half:])], axis=-1)
    return sw.reshape(w.shape)


def _twice_per_head(w, head_dim):
    n_heads = w.shape[1] // head_dim
    w3 = w.reshape(w.shape[0], n_heads, head_dim)
    return jnp.concatenate([w3, w3], axis=-1).reshape(w.shape[0], 2 * w.shape[1])


def _rope_tables(seq, half, theta, lanes_before, lanes_after, pass_through, scale):
    inv = theta ** (-jnp.arange(half, dtype=F32) / half)
    ang = jnp.arange(seq).astype(F32)[:, None] * inv[None, :]
    cos, sin = jnp.cos(ang), jnp.sin(ang)
    ones = lambda w, v: jnp.full((seq, w), v, F32)
    ct = jnp.concatenate([ones(lanes_before, pass_through), cos, cos, ones(lanes_after, pass_through)], axis=-1)
    st = jnp.concatenate([ones(lanes_before, 0.0), -sin, sin, ones(lanes_after, 0.0)], axis=-1)
    return ct * scale, st * scale


def _prep_even(a_w_in, a_q_norm, a_kv_norm, a_w_uq, a_w_ukv, a_w_out):
    o1 = MLA_Q_LORA
    o2 = o1 + MLA_KV_LORA
    o3 = o2 + MLA_ROPE
    nb = DIL_HEADS * DIL_HD
    bf = lambda t: t.astype(BF16)
    w = {}
    w["wcq"] = bf(a_w_in[:, :o1])
    w["wckv"] = bf(a_w_in[:, o1:o2])
    kr = a_w_in[:, o2:o3]
    krs = jnp.concatenate([kr[:, MLA_ROPE // 2:], kr[:, :MLA_ROPE // 2]], axis=-1)
    pad_slot = lambda t: jnp.pad(t, ((0, 0), (MLA_NOPE, MLA_SLOT - MLA_NOPE - MLA_ROPE)))
    w["wkr"] = bf(jnp.concatenate([pad_slot(kr), pad_slot(krs)], axis=-1))
    qb = a_w_in[:, o3:o3 + nb]
    kb = a_w_in[:, o3 + nb:o3 + 2 * nb]
    rd2 = DIL_HD // ROPE_FRACTION // 2
    w["wqb"] = bf(jnp.concatenate([qb, _rope_partner_cols(qb, DIL_HD, rd2)], axis=-1))
    w["wkb"] = bf(jnp.concatenate([kb, _rope_partner_cols(kb, DIL_HD, rd2)], axis=-1))
    w["wvb"] = bf(a_w_in[:, o3 + 2 * nb:])
    w["qn"] = a_q_norm.reshape(1, -1)
    w["kvn"] = a_kv_norm.reshape(1, -1)
    uq = a_w_uq.reshape(MLA_Q_LORA, MLA_HEADS, MLA_NOPE + MLA_ROPE)
    zpad = jnp.zeros((MLA_Q_LORA, MLA_HEADS, MLA_SLOT - MLA_NOPE - MLA_ROPE), F32)
    uq1 = jnp.concatenate([uq, zpad], axis=-1).reshape(MLA_Q_LORA, MLA_HEADS * MLA_SLOT)
    r = uq[..., MLA_NOPE:]
    uq2 = jnp.concatenate([jnp.zeros_like(uq[..., :MLA_NOPE]), r[..., MLA_ROPE // 2:], r[..., :MLA_ROPE // 2],
                           zpad], axis=-1).reshape(MLA_Q_LORA, MLA_HEADS * MLA_SLOT)
    w["wuq"] = bf(jnp.concatenate([uq1, uq2], axis=-1))
    ukv = a_w_ukv.reshape(MLA_KV_LORA, MLA_HEADS, MLA_NOPE + MLA_V)
    kpad = jnp.zeros((MLA_KV_LORA, MLA_HEADS, MLA_SLOT - MLA_NOPE), F32)
    w["wk"] = bf(jnp.concatenate([ukv[..., :MLA_NOPE], kpad], axis=-1).reshape(MLA_KV_LORA, -1))
    w["wv"] = bf(ukv[..., MLA_NOPE:].reshape(MLA_KV_LORA, MLA_HEADS * MLA_V))
    w["wout_a"] = bf(a_w_out[:MLA_HEADS * MLA_V])
    w["wout_b"] = bf(a_w_out[MLA_HEADS * MLA_V:])
    return w


def _prep_odd(c_w_in, c_w_out):
    e1 = C_HEADS * C_HD
    e2 = e1 + C_KV_HEADS * C_HD
    rd2 = C_HD // ROPE_FRACTION // 2
    q, k = c_w_in[:, :e1], c_w_in[:, e1:e2]
    k2 = lambda t: _twice_per_head(t, C_HD)
    return {
        "wq": jnp.concatenate([q, _rope_partner_cols(q, C_HD, rd2)], axis=-1).astype(BF16),
        "wk": jnp.concatenate([k2(k), k2(_rope_partner_cols(k, C_HD, rd2))], axis=-1).astype(BF16),
        "wv": k2(c_w_in[:, e2:]).astype(BF16),
        "wout": c_w_out.astype(BF16),
    }


def _split_kernel(w_ref, hi_ref, lo_ref):
    hi_ref[...], lo_ref[...] = _split_bf16(w_ref[...])


def _split_hi_lo(w):
    flat = w.reshape(-1, LANES)
    rows = flat.shape[0]
    blk = min(rows, 1024)
    spec = pl.BlockSpec((blk, LANES), lambda i: (i, 0))
    hi, lo = pl.pallas_call(
        _split_kernel,
        grid=(rows // blk,),
        in_specs=[spec], out_specs=[spec, spec],
        out_shape=[jax.ShapeDtypeStruct(flat.shape, BF16)] * 2,
        compiler_params=_params("parallel"),
        name="split_hi_lo",
    )(flat)
    return hi.reshape(w.shape), lo.reshape(w.shape)


def _prep_peer(peer_w_q, peer_sub_keys, peer_u, peer_v):
    wq_hi, wq_lo = _split_hi_lo(jnp.swapaxes(peer_w_q, 1, 2))
    keys_hi, keys_lo = _split_hi_lo(peer_sub_keys)
    return {
        "wq_hi": wq_hi, "wq_lo": wq_lo, "keys_hi": keys_hi, "keys_lo": keys_lo,
        "u": peer_u.astype(BF16), "vt": jnp.swapaxes(peer_v, 1, 2).astype(BF16),
    }


def _tables_mla(seq):
    scale = (MLA_NOPE + MLA_ROPE) ** -0.5
    pad = MLA_SLOT - MLA_NOPE - MLA_ROPE
    cq, sq = _rope_tables(seq, MLA_ROPE // 2, MLA_THETA, MLA_NOPE, pad, 1.0, scale)
    ck, sk = _rope_tables(seq, MLA_ROPE // 2, MLA_THETA, MLA_NOPE, pad, 0.0, 1.0)
    return jnp.stack([cq, sq, ck, sk])


def _tables_partial(seq, head_dim, q_scale):
    half = head_dim // ROPE_FRACTION // 2
    cq, sq = _rope_tables(seq, half, ROPE_THETA, 0, head_dim - 2 * half, 1.0, q_scale)
    ck, sk = _rope_tables(seq, half, ROPE_THETA, 0, head_dim - 2 * half, 1.0, 1.0)
    two = lambda t: jnp.concatenate([t, t], axis=-1)
    return jnp.stack([two(cq), two(sq), two(ck), two(sk)])


def _peer_layer(x, pw, i, g, b):
    stats = _peer_select(x, pw["wq_hi"][i], pw["wq_lo"][i], pw["keys_hi"][i], pw["keys_lo"][i])
    return _peer_experts(x, stats, pw["u"][i], pw["vt"][i], g, b)


def _mixer_even(x, w, n_seq, seq, g, b):
    outs = _inproj_even(x, w, _tables_mla(seq), _tables_partial(seq, DIL_HD, DIL_HD ** -0.5), seq)
    qa, ka, va = outs[:3]
    oa = _mla_attention(qa, ka, va, n_seq, seq)
    pats = []
    for p, (window, dil) in enumerate(DIL_PATTERNS):
        qd, kd, vd = outs[3 + 3 * p:6 + 3 * p]
        pats.append(_banded_attention(qd, kd, vd, n_seq, seq, dil, (window // 2) // dil, lambda m: m, emit_lse=True))
    return _outproj_even(oa, pats, w["wout_a"], w["wout_b"], x, g, b)


def _mixer_odd(x, w, sink, n_seq, seq, g, b):
    q, k, v = _inproj_odd(x, w, _tables_partial(seq, C_HD, C_HD ** -0.5), seq)
    pairs_per_group = C_HEADS // C_KV_HEADS // 2
    (o,) = _banded_attention(q, k, v, n_seq, seq, 1, C_RADIUS, lambda m: m // pairs_per_group, sink=sink)
    return _outproj_odd(o, w["wout"], x, g, b)


def _trunk(x3, p4, prm):
    n_seq, seq, _ = x3.shape
    x = x3.reshape(n_seq * seq, D_MODEL)
    p = p4.reshape(DEPTH, n_seq * seq, PLE_DIM)
    row = lambda t: t.reshape(1, -1)
    for i in range(DEPTH):
        g, b = row(prm["ln_mix_g"][i]), row(prm["ln_mix_b"][i])
        if i % 2 == 0:
            x = _mixer_even(x, prm["even"], n_seq, seq, g, b)
        else:
            x = _mixer_odd(x, prm["odd"], prm["sink"], n_seq, seq, g, b)
        x = _peer_layer(x, prm["peer"], i, row(prm["ln_ffn_g"][i]), row(prm["ln_ffn_b"][i]))
        x = _ple(x, p[i], prm["ple_proj"][i], prm["ple_gate"][i])
    return x.reshape(n_seq, seq, D_MODEL)


def kernel(x_prompt, x_sample, p_prompt, p_sample, a_w_in, a_q_norm, a_kv_norm, a_w_uq, a_w_ukv, a_w_out,
           c_w_in, c_sink, c_w_out, ln_mix_g, ln_mix_b, ln_ffn_g, ln_ffn_b, peer_w_q, peer_sub_keys,
           peer_u, peer_v, ple_proj, ple_gate):
    prm = {
        "even": _prep_even(a_w_in[0], a_q_norm[0], a_kv_norm[0], a_w_uq[0], a_w_ukv[0], a_w_out[0]),
        "odd": _prep_odd(c_w_in[0], c_w_out[0]),
        "sink": c_sink[0],
        "ln_mix_g": ln_mix_g, "ln_mix_b": ln_mix_b, "ln_ffn_g": ln_ffn_g, "ln_ffn_b": ln_ffn_b,
        "peer": _prep_peer(peer_w_q, peer_sub_keys, peer_u, peer_v),
        "ple_proj": ple_proj.astype(BF16),
        "ple_gate": ple_gate.astype(BF16),
    }
    y_prompt = _trunk(x_prompt, p_prompt, prm)
    y_sample = _trunk(x_sample, p_sample, prm)
    return (y_prompt, y_sample)
```

```python
import functools

import jax
import jax.numpy as jnp
import numpy as np
from jax import lax
from jax.experimental import pallas as pl
from jax.experimental.pallas import tpu as pltpu

F32 = jnp.float32
BF16 = jnp.bfloat16

D_MODEL = 1024
DEPTH = 2
PLE_DIM = 256
MLA_HEADS = 8
MLA_Q_LORA = 384
MLA_KV_LORA = 256
MLA_NOPE = 64
MLA_ROPE = 32
MLA_V = 64
MLA_THETA = 10000.0
MLA_SLOT = 128
DIL_HEADS = 8
DIL_HD = 64
DIL_PATTERNS = ((128, 1), (512, 4), (2048, 16))
DIL_W = DIL_HEADS * DIL_HD
C_HEADS = 16
C_KV_HEADS = 4
C_HD = 64
C_RADIUS = 128
ROPE_THETA = 500000.0
ROPE_FRACTION = 4
PEER_HEADS = 8
PEER_N_KEYS = 128
PEER_N_EXPERTS = PEER_N_KEYS * PEER_N_KEYS
PEER_D_KEY = 128
PEER_TOPK = 16
DN_ALPHA = (2 * DEPTH) ** 0.25
LN_EPS = 1e-5
RMS_EPS = 1e-6
NEG = -1e30
INV_SQRT2 = 0.7071067811865476
NO_KEY_QUALIFIES = 1.0

LANES = 128
SUBLANES = 8
VMEM_LIMIT_BYTES = 56 * 1024 * 1024

TM = 512
TQ_MLA = 512
TK_MLA = 1024
TQ_BAND = 256
T_HALO = 128
TM_SEL = 512
TM_PEER = 512
EB_PEER = 1024

_NT = (((1,), (1,)), ((), ()))


def _params(*sem):
    return pltpu.CompilerParams(dimension_semantics=sem, vmem_limit_bytes=VMEM_LIMIT_BYTES)


def _layer_norm(y, g, b):
    mu = jnp.mean(y, axis=-1, keepdims=True)
    d = y - mu
    var = jnp.mean(d * d, axis=-1, keepdims=True)
    return d * lax.rsqrt(var + LN_EPS) * g + b


def _rms_norm(y, g):
    return y * lax.rsqrt(jnp.mean(y * y, axis=-1, keepdims=True) + RMS_EPS) * g


def _tile_lanes(t, reps):
    return jnp.concatenate([t] * reps, axis=-1)


def _tile_rows(t, reps):
    return jnp.concatenate([t] * reps, axis=0)


def _low_half_lanes():
    return lax.broadcasted_iota(jnp.int32, (1, LANES), 1) < LANES // 2


def _const_spec(shape):
    nd = len(shape)
    return pl.BlockSpec(shape, lambda *_: (0,) * nd)


def _inproj_even_kernel(x_ref, wcq_ref, wckv_ref, wkr_ref, wqb_ref, wkb_ref, wvb_ref,
                        qn_ref, kvn_ref, wuq_ref, wk_ref, wv_ref, ta_ref, tb_ref,
                        qa_ref, ka_ref, va_ref, qb_ref, kb_ref, vb_ref,
                        q4_ref, k4_ref, v4_ref, q16_ref, k16_ref, v16_ref, perm_ref):
    xb = x_ref[...].astype(BF16)
    dot = functools.partial(jnp.dot, preferred_element_type=F32)
    cq = _rms_norm(dot(xb, wcq_ref[...]), qn_ref[...]).astype(BF16)
    q12 = dot(cq, wuq_ref[...])
    cq_t = _tile_lanes(ta_ref[0], MLA_HEADS)
    sq_t = _tile_lanes(ta_ref[1], MLA_HEADS)
    qa_ref[...] = (q12[:, :1024] * cq_t + q12[:, 1024:] * sq_t).astype(BF16)
    ckv = _rms_norm(dot(xb, wckv_ref[...]), kvn_ref[...]).astype(BF16)
    kr12 = dot(xb, wkr_ref[...])
    kr = kr12[:, :LANES] * ta_ref[2] + kr12[:, LANES:] * ta_ref[3]
    ka_ref[...] = (dot(ckv, wk_ref[...]) + _tile_lanes(kr, MLA_HEADS)).astype(BF16)
    va_ref[...] = dot(ckv, wv_ref[...]).astype(BF16)
    def emit(val, nat_ref, views):
        nat_ref[...] = val.astype(BF16)
        for lt in range(DIL_W // LANES):
            perm_ref[lt] = val[:, lt * LANES:(lt + 1) * LANES]
        for d, ref in views:
            for r in range(d):
                for lt in range(DIL_W // LANES):
                    c0 = r * DIL_W + lt * LANES
                    ref[:, c0:c0 + LANES] = perm_ref[lt, pl.ds(r, TM // d, stride=d), :].astype(BF16)

    q12b = dot(xb, wqb_ref[...])
    emit(q12b[:, :512] * _tile_lanes(tb_ref[0], 4) + q12b[:, 512:] * _tile_lanes(tb_ref[1], 4),
         qb_ref, ((4, q4_ref), (16, q16_ref)))
    k12b = dot(xb, wkb_ref[...])
    emit(k12b[:, :512] * _tile_lanes(tb_ref[2], 4) + k12b[:, 512:] * _tile_lanes(tb_ref[3], 4),
         kb_ref, ((4, k4_ref), (16, k16_ref)))
    emit(dot(xb, wvb_ref[...]), vb_ref, ((4, v4_ref), (16, v16_ref)))


def _inproj_even(x, w, ta, tb, seq):
    n = x.shape[0]
    pos_blocks = seq // TM
    row = lambda i: (i, 0)
    tab = lambda i: (0, i % pos_blocks, 0)
    wnames = ("wcq", "wckv", "wkr", "wqb", "wkb", "wvb", "qn", "kvn", "wuq", "wk", "wv")
    ws = [w[k] for k in wnames]
    out_w = (1024, 1024, 512, 512, 512, 512)
    views = [4, 4, 4, 16, 16, 16]
    return pl.pallas_call(
        _inproj_even_kernel,
        grid=(n // TM,),
        in_specs=[pl.BlockSpec((TM, D_MODEL), row)] + [_const_spec(a.shape) for a in ws]
        + [pl.BlockSpec((4, TM, LANES), tab), pl.BlockSpec((4, TM, LANES), tab)],
        out_specs=[pl.BlockSpec((TM, c), row) for c in out_w]
        + [pl.BlockSpec((TM // d, d * DIL_W), row) for d in views],
        out_shape=[jax.ShapeDtypeStruct((n, c), BF16) for c in out_w]
        + [jax.ShapeDtypeStruct((n // d, d * DIL_W), BF16) for d in views],
        scratch_shapes=[pltpu.VMEM((DIL_W // LANES, TM, LANES), F32)],
        compiler_params=_params("parallel"),
        name="inproj_even",
    )(x, *ws, ta, tb)


def _mla_kernel(q_ref, k_ref, v_ref, o_ref, m_ref, l_ref, acc_ref):
    ki = pl.program_id(2)
    low = _low_half_lanes()
    reps = TK_MLA // LANES

    @pl.when(ki == 0)
    def _init():
        m_ref[...] = jnp.full(m_ref.shape, NEG, F32)
        l_ref[...] = jnp.zeros(l_ref.shape, F32)
        acc_ref[...] = jnp.zeros(acc_ref.shape, F32)

    for j in range(MLA_HEADS // 2):
        vslot = v_ref[:, j * LANES:(j + 1) * LANES]
        prod, alphas = [], []
        for half in range(2):
            h = 2 * j + half
            qh = q_ref[:, h * MLA_SLOT:(h + 1) * MLA_SLOT]
            kh = k_ref[:, h * MLA_SLOT:(h + 1) * MLA_SLOT]
            s = lax.dot_general(qh, kh, _NT, preferred_element_type=F32)
            m_prev = m_ref[h]
            m_new = jnp.maximum(m_prev, jnp.max(s, axis=-1, keepdims=True))
            alpha = jnp.exp(m_prev - m_new)
            p = jnp.exp(s - _tile_lanes(m_new, reps))
            l_ref[h] = alpha * l_ref[h] + jnp.sum(p, axis=-1, keepdims=True)
            m_ref[h] = m_new
            prod.append(jnp.dot(p.astype(BF16), vslot, preferred_element_type=F32))
            alphas.append(alpha)
        acc_ref[j] = (jnp.where(low, alphas[0], alphas[1]) * acc_ref[j]
                      + jnp.where(low, prod[0], prod[1]))

    @pl.when(ki == pl.num_programs(2) - 1)
    def _fin():
        outs = [acc_ref[j] / jnp.where(low, l_ref[2 * j], l_ref[2 * j + 1]) for j in range(MLA_HEADS // 2)]
        o_ref[...] = jnp.concatenate(outs, axis=-1).astype(BF16)


def _mla_attention(q, k, v, n_seq, seq):
    n = q.shape[0]
    nq, nk = seq // TQ_MLA, seq // TK_MLA
    return pl.pallas_call(
        _mla_kernel,
        grid=(n_seq, nq, nk),
        in_specs=[pl.BlockSpec((TQ_MLA, 1024), lambda b, i, j: (b * nq + i, 0)),
                  pl.BlockSpec((TK_MLA, 1024), lambda b, i, j: (b * nk + j, 0)),
                  pl.BlockSpec((TK_MLA, 512), lambda b, i, j: (b * nk + j, 0))],
        out_specs=pl.BlockSpec((TQ_MLA, 512), lambda b, i, j: (b * nq + i, 0)),
        out_shape=jax.ShapeDtypeStruct((n, 512), BF16),
        scratch_shapes=[pltpu.VMEM((MLA_HEADS, TQ_MLA, LANES), F32),
                        pltpu.VMEM((MLA_HEADS, TQ_MLA, LANES), F32),
                        pltpu.VMEM((MLA_HEADS // 2, TQ_MLA, LANES), F32)],
        compiler_params=_params("parallel", "parallel", "arbitrary"),
        name="mla_attention",
    )(q, k, v)


def _band_bias(radius, tq):
    row = np.arange(tq)[:, None]
    col = np.arange(tq + 2 * T_HALO)[None, :]
    d = col - T_HALO - row
    return np.where(np.abs(d) <= radius, 0.0, NEG).astype(np.float32)


def _banded_kernel(*refs, tq, n_pairs, slot_of_pair, has_sink, emit_lse):
    refs = list(refs)
    sink_ref = refs.pop(0) if has_sink else None
    q_ref, kp_ref, kc_ref, kn_ref, vp_ref, vc_ref, vn_ref, bias_ref = refs[:8]
    o_ref = refs[8]
    lse_ref = refs[9] if emit_lse else None
    qi = pl.program_id(2)
    nb = pl.num_programs(2)
    col = lax.broadcasted_iota(jnp.int32, (1, tq + 2 * T_HALO), 1)
    no_prev = jnp.where(qi == 0, NEG, 0.0)
    no_next = jnp.where(qi == nb - 1, NEG, 0.0)
    bias = (bias_ref[...] + jnp.where(col < T_HALO, no_prev, 0.0)
            + jnp.where(col >= tq + T_HALO, no_next, 0.0))
    low = _low_half_lanes()
    bias2 = _tile_rows(bias, 2)
    for m in range(n_pairs):
        sl = slice(slot_of_pair(m) * LANES, (slot_of_pair(m) + 1) * LANES)
        kcat = jnp.concatenate([kp_ref[:, sl], kc_ref[:, sl], kn_ref[:, sl]], axis=0)
        vcat = jnp.concatenate([vp_ref[:, sl], vc_ref[:, sl], vn_ref[:, sl]], axis=0)
        qp = q_ref[:, m * LANES:(m + 1) * LANES]
        zero = jnp.zeros_like(qp)
        q2 = jnp.concatenate([jnp.where(low, qp, zero), jnp.where(low, zero, qp)], axis=0)
        s = lax.dot_general(q2, kcat, _NT, preferred_element_type=F32) + bias2
        mx = jnp.max(s, axis=-1, keepdims=True)
        if has_sink:
            first = lax.broadcasted_iota(jnp.int32, (2 * tq, 1), 0) < tq
            sk = jnp.where(first, sink_ref[2 * m], sink_ref[2 * m + 1])
            mx = jnp.maximum(mx, sk)
        p = jnp.exp(s - mx)
        den = jnp.sum(p, axis=-1, keepdims=True)
        if has_sink:
            den = den + jnp.exp(sk - mx)
        res = jnp.dot(p.astype(BF16), vcat, preferred_element_type=F32) / den
        o_ref[:, m * LANES:(m + 1) * LANES] = jnp.where(low, res[:tq], res[tq:]).astype(BF16)
        if emit_lse:
            lse = mx + jnp.log(den)
            lse_ref[:, m * LANES:(m + 1) * LANES] = jnp.where(low, lse[:tq], lse[tq:])


def _banded_attention(q, k, v, n_seq, seq, dil, radius, slot_of_pair, sink=None, emit_lse=False):
    n, qw = q.shape[0] * dil, q.shape[1] // dil
    kw = k.shape[1] // dil
    sub = seq // dil
    tq = min(TQ_BAND, sub)
    nb = sub // tq
    hq = tq // T_HALO
    nh = sub // T_HALO
    cur = lambda b, r, i: (b * nb + i, r)
    prev = lambda b, r, i: (b * nh + jnp.maximum(i * hq - 1, 0), r)
    nxt = lambda b, r, i: (b * nh + jnp.minimum((i + 1) * hq, nh - 1), r)
    halo = lambda f: pl.BlockSpec((T_HALO, kw), f)
    body = pl.BlockSpec((tq, kw), cur)
    in_specs = [pl.BlockSpec((tq, qw), cur),
                halo(prev), body, halo(nxt), halo(prev), body, halo(nxt),
                pl.BlockSpec((tq, tq + 2 * T_HALO), lambda b, r, i: (0, 0))]
    args = [q, k, k, k, v, v, v, jnp.asarray(_band_bias(radius, tq))]
    if sink is not None:
        in_specs = [pl.BlockSpec(memory_space=pltpu.SMEM)] + in_specs
        args = [sink] + args
    out_specs = [pl.BlockSpec((tq, qw), cur)]
    out_shape = [jax.ShapeDtypeStruct((n // dil, dil * qw), BF16)]
    if emit_lse:
        out_specs.append(pl.BlockSpec((tq, qw), cur))
        out_shape.append(jax.ShapeDtypeStruct((n // dil, dil * qw), F32))
    return pl.pallas_call(
        functools.partial(_banded_kernel, tq=tq, n_pairs=qw // LANES, slot_of_pair=slot_of_pair,
                          has_sink=sink is not None, emit_lse=emit_lse),
        grid=(n_seq, dil, nb),
        in_specs=in_specs, out_specs=out_specs, out_shape=out_shape,
        compiler_params=_params("parallel", "parallel", "parallel"),
        name="banded_attention",
    )(*args)


def _outproj_even_kernel(oa_ref, o1_ref, l1_ref, o4_ref, l4_ref, o16_ref, l16_ref, wa_ref, wb_ref,
                         x_ref, g_ref, b_ref, y_ref, po4_ref, pl4_ref, po16_ref, pl16_ref):
    def natural(src_ref, dst_ref, d):
        for r in range(d):
            for lt in range(DIL_W // LANES):
                c0 = r * DIL_W + lt * LANES
                dst_ref[lt, pl.ds(r, TM // d, stride=d), :] = src_ref[:, c0:c0 + LANES].astype(F32)
        return jnp.concatenate([dst_ref[lt] for lt in range(DIL_W // LANES)], axis=-1)

    l1, o1 = l1_ref[...], o1_ref[...].astype(F32)
    l2, o2 = natural(l4_ref, pl4_ref, 4), natural(o4_ref, po4_ref, 4)
    l3, o3 = natural(l16_ref, pl16_ref, 16), natural(o16_ref, po16_ref, 16)
    mx = jnp.maximum(jnp.maximum(l1, l2), l3)
    e1, e2, e3 = jnp.exp(l1 - mx), jnp.exp(l2 - mx), jnp.exp(l3 - mx)
    ob = (e1 * o1 + e2 * o2 + e3 * o3) / (e1 + e2 + e3)
    mix = (jnp.dot(oa_ref[...], wa_ref[...], preferred_element_type=F32)
           + jnp.dot(ob.astype(BF16), wb_ref[...], preferred_element_type=F32))
    y_ref[...] = _layer_norm(DN_ALPHA * x_ref[...] + mix, g_ref[...], b_ref[...])


def _outproj_even(oa, pats, wa, wb, x, g, b):
    n = x.shape[0]
    row = lambda i: (i, 0)
    pat_specs, pat_args = [], []
    for (_, d), (o, lse) in zip(DIL_PATTERNS, pats):
        pat_specs += [pl.BlockSpec((TM // d, d * DIL_W), row)] * 2
        pat_args += [o, lse]
    return pl.pallas_call(
        _outproj_even_kernel,
        grid=(n // TM,),
        in_specs=[pl.BlockSpec((TM, DIL_W), row)] + pat_specs
        + [_const_spec(wa.shape), _const_spec(wb.shape),
           pl.BlockSpec((TM, D_MODEL), row), _const_spec(g.shape), _const_spec(b.shape)],
        out_specs=pl.BlockSpec((TM, D_MODEL), row),
        out_shape=jax.ShapeDtypeStruct((n, D_MODEL), F32),
        scratch_shapes=[pltpu.VMEM((DIL_W // LANES, TM, LANES), F32)] * 4,
        compiler_params=_params("parallel"),
        name="outproj_even",
    )(oa, *pat_args, wa, wb, x, g, b)


def _outproj_odd_kernel(o_ref, w_ref, x_ref, g_ref, b_ref, y_ref):
    mix = jnp.dot(o_ref[...], w_ref[...], preferred_element_type=F32)
    y_ref[...] = _layer_norm(DN_ALPHA * x_ref[...] + mix, g_ref[...], b_ref[...])


def _outproj_odd(o, w, x, g, b):
    n = x.shape[0]
    row = lambda i: (i, 0)
    return pl.pallas_call(
        _outproj_odd_kernel,
        grid=(n // TM,),
        in_specs=[pl.BlockSpec((TM, o.shape[1]), row), _const_spec(w.shape),
                  pl.BlockSpec((TM, D_MODEL), row), _const_spec(g.shape), _const_spec(b.shape)],
        out_specs=pl.BlockSpec((TM, D_MODEL), row),
        out_shape=jax.ShapeDtypeStruct((n, D_MODEL), F32),
        compiler_params=_params("parallel"),
        name="outproj_odd",
    )(o, w, x, g, b)


def _inproj_odd_kernel(x_ref, wq_ref, wk_ref, wv_ref, tb_ref, q_ref, k_ref, v_ref):
    xb = x_ref[...].astype(BF16)
    dot = functools.partial(jnp.dot, preferred_element_type=F32)
    q12 = dot(xb, wq_ref[...])
    q_ref[...] = (q12[:, :1024] * _tile_lanes(tb_ref[0], 8)
                  + q12[:, 1024:] * _tile_lanes(tb_ref[1], 8)).astype(BF16)
    k12 = dot(xb, wk_ref[...])
    k_ref[...] = (k12[:, :512] * _tile_lanes(tb_ref[2], 4)
                  + k12[:, 512:] * _tile_lanes(tb_ref[3], 4)).astype(BF16)
    v_ref[...] = dot(xb, wv_ref[...]).astype(BF16)


def _inproj_odd(x, w, tb, seq):
    n = x.shape[0]
    pos_blocks = seq // TM
    row = lambda i: (i, 0)
    ws = [w["wq"], w["wk"], w["wv"]]
    out_w = (1024, 512, 512)
    return pl.pallas_call(
        _inproj_odd_kernel,
        grid=(n // TM,),
        in_specs=[pl.BlockSpec((TM, D_MODEL), row)] + [_const_spec(a.shape) for a in ws]
        + [pl.BlockSpec((4, TM, LANES), lambda i: (0, i % pos_blocks, 0))],
        out_specs=[pl.BlockSpec((TM, c), row) for c in out_w],
        out_shape=[jax.ShapeDtypeStruct((n, c), BF16) for c in out_w],
        compiler_params=_params("parallel"),
        name="inproj_odd",
    )(x, *ws, tb)


def _sorting_network(n):
    pairs = []

    def merge(lo, hi, r):
        step = r * 2
        if step < hi - lo:
            merge(lo, hi, step)
            merge(lo + r, hi, step)
            pairs.extend((i, i + r) for i in range(lo + r, hi - r, step))
        else:
            pairs.append((lo, lo + r))

    def sort(lo, hi):
        if hi - lo >= 1:
            mid = lo + (hi - lo) // 2
            sort(lo, mid)
            sort(mid + 1, hi)
            merge(lo, hi, 1)

    sort(0, n - 1)
    return pairs


def _top16_sorted(s):
    k = PEER_TOPK
    w = [s[g * SUBLANES:(g + 1) * SUBLANES, :] for g in range(s.shape[0] // SUBLANES)]
    w += [jnp.full_like(w[0], NEG)] * (k - len(w))

    def exchange(i, j):
        w[i], w[j] = jnp.maximum(w[i], w[j]), jnp.minimum(w[i], w[j])

    for i, j in _sorting_network(k):
        exchange(i, j)
    shift = SUBLANES // 2
    while shift >= 1:
        other = [pltpu.roll(t, shift, 0) for t in w]
        w = [jnp.maximum(w[p], other[k - 1 - p]) for p in range(k)]
        d = k // 2
        while d >= 1:
            for i in range(k):
                if i & d == 0:
                    exchange(i, i + d)
            d //= 2
        shift //= 2
    return w


def _split_bf16(t):
    hi = t.astype(BF16)
    return hi, (t - hi.astype(F32)).astype(BF16)


def _dot3(a_hi, a_lo, b_hi, b_lo, dims):
    d = functools.partial(lax.dot_general, dimension_numbers=dims, preferred_element_type=F32)
    return d(a_hi, b_hi) + (d(a_hi, b_lo) + d(a_lo, b_hi))


def _peer_select_kernel(x_ref, wqh_ref, wql_ref, kh_ref, kl_ref, e2_ref, thr_ref, e1_ref,
                        qh_ref, ql_ref, a_ref, b_ref):
    h = pl.program_id(1)

    @pl.when(h == 0)
    def _project():
        xh, xl = _split_bf16(x_ref[...])
        q = _dot3(wqh_ref[...], wql_ref[...], xh, xl, _NT)
        qh_ref[...], ql_ref[...] = _split_bf16(q)

    half = PEER_D_KEY // 2
    row0 = pl.multiple_of(h * PEER_D_KEY, PEER_D_KEY)
    mm = (((1,), (0,)), ((), ()))
    s1 = _dot3(kh_ref[0, 0], kl_ref[0, 0], qh_ref[pl.ds(row0, half), :], ql_ref[pl.ds(row0, half), :], mm)
    s2 = _dot3(kh_ref[0, 1], kl_ref[0, 1], qh_ref[pl.ds(row0 + half, half), :],
               ql_ref[pl.ds(row0 + half, half), :], mm)
    for lt in range(TM_SEL // LANES):
        ls = slice(lt * LANES, (lt + 1) * LANES)
        s1t, s2t = s1[:, ls], s2[:, ls]
        a_rep, b_rep = _top16_sorted(s1t), _top16_sorted(s2t)
        for p in range(PEER_TOPK):
            a_ref[p:p + 1, ls] = a_rep[p][0:1]
            b_ref[p:p + 1, ls] = b_rep[p][0:1]
        a16, b16 = a_ref[:, ls], b_ref[:, ls]
        a0, b0 = a16[0:1], b16[0:1]
        cand = [a0 + b16]
        cand += [a16[i:i + 1] + b16[0:8] for i in range(1, 8)]
        cand += [a16[8:16] + b0]
        cand = jnp.concatenate(cand, axis=0)
        v16 = _top16_sorted(cand)[PEER_TOPK - 1][0:1]
        z = jnp.sum(jnp.where(cand >= v16, jnp.exp(cand - (a0 + b0)), 0.0), axis=0, keepdims=True)
        half_inv_z = 0.5 / z
        e2_sorted = jnp.exp(b16 - b0) * half_inv_z
        thr = jnp.full(s1t.shape, NO_KEY_QUALIFIES, F32)
        for i in range(PEER_TOPK):
            t_i = jnp.min(jnp.where(a16[i:i + 1] + b16 >= v16, e2_sorted, NO_KEY_QUALIFIES), axis=0, keepdims=True)
            thr = jnp.where(s1t == _tile_rows(a_rep[i], PEER_N_KEYS // SUBLANES), t_i, thr)
        e2_ref[:, ls] = jnp.exp(s2t - b0) * half_inv_z
        thr_ref[:, ls] = thr
        e1_ref[:, ls] = jnp.exp(s1t - a0)


def _peer_select(x, wq_hi, wq_lo, keys_hi, keys_lo):
    n = x.shape[0]
    spec = pl.BlockSpec((PEER_N_KEYS, TM_SEL), lambda i, h: (h, i))
    shape = lambda dt: jax.ShapeDtypeStruct((PEER_HEADS * PEER_N_KEYS, n), dt)
    kspec = pl.BlockSpec((1, 2, PEER_N_KEYS, PEER_D_KEY // 2), lambda i, h: (h, 0, 0, 0))
    return pl.pallas_call(
        _peer_select_kernel,
        grid=(n // TM_SEL, PEER_HEADS),
        in_specs=[pl.BlockSpec((TM_SEL, D_MODEL), lambda i, h: (i, 0)),
                  _const_spec(wq_hi.shape), _const_spec(wq_lo.shape), kspec, kspec],
        out_specs=[spec] * 3,
        out_shape=[shape(F32)] * 3,
        scratch_shapes=[pltpu.VMEM((PEER_HEADS * PEER_D_KEY, TM_SEL), BF16),
                        pltpu.VMEM((PEER_HEADS * PEER_D_KEY, TM_SEL), BF16),
                        pltpu.VMEM((PEER_TOPK, TM_SEL), F32),
                        pltpu.VMEM((PEER_TOPK, TM_SEL), F32)],
        compiler_params=_params("parallel", "arbitrary"),
        name="peer_select",
    )(x, wq_hi, wq_lo, keys_hi, keys_lo)


def _peer_expert_kernel(x_ref, e2_ref, thr_ref, e1_ref, u_ref, vt_ref, g_ref, b_ref, y_ref,
                        xb_ref, a_ref, h_ref, acc_ref):
    eb = pl.program_id(1)
    n_i1 = EB_PEER // PEER_N_KEYS
    halves = 2
    rows_half = EB_PEER // halves
    n_r = 2
    slab = PEER_N_KEYS // 2

    @pl.when(eb == 0)
    def _init():
        xb_ref[...] = x_ref[...].astype(BF16)
        acc_ref[...] = jnp.zeros(acc_ref.shape, F32)

    for hh in range(halves):
        rs = slice(hh * rows_half, (hh + 1) * rows_half)
        a_ref[rs, :] = lax.dot_general(u_ref[rs, :], xb_ref[...], _NT, preferred_element_type=F32)
    i1_base = pl.multiple_of(eb * n_i1, SUBLANES)
    for hh in range(halves):
        for lt in range(TM_PEER // LANES):
            ls = slice(lt * LANES, (lt + 1) * LANES)
            for rp in range(n_i1 // halves // n_r):
                r0 = hh * (n_i1 // halves) + n_r * rp
                for sb in range(PEER_N_KEYS // slab):
                    gate = [jnp.zeros((slab, LANES), F32)] * n_r
                    for h in range(PEER_HEADS):
                        t8 = thr_ref[pl.ds(h * PEER_N_KEYS + i1_base, n_i1), ls]
                        e8 = e1_ref[pl.ds(h * PEER_N_KEYS + i1_base, n_i1), ls]
                        e2t = e2_ref[h * PEER_N_KEYS + sb * slab:h * PEER_N_KEYS + (sb + 1) * slab, ls]
                        for t in range(n_r):
                            r = r0 + t
                            gate[t] = gate[t] + jnp.where(e2t >= t8[r:r + 1], e2t, 0.0) * e8[r:r + 1]
                    for t in range(n_r):
                        rows = slice((r0 + t) * PEER_N_KEYS + sb * slab, (r0 + t) * PEER_N_KEYS + (sb + 1) * slab)
                        a = a_ref[rows, ls]
                        h_ref[rows, ls] = (a * (1.0 + lax.erf(a * INV_SQRT2)) * gate[t]).astype(BF16)
        rs = slice(hh * rows_half, (hh + 1) * rows_half)
        acc_ref[...] += jnp.dot(vt_ref[:, rs], h_ref[rs, :], preferred_element_type=F32)

    @pl.when(eb == pl.num_programs(1) - 1)
    def _fin():
        y = DN_ALPHA * x_ref[...] + acc_ref[...].T
        y_ref[...] = _layer_norm(y, g_ref[...], b_ref[...])


def _peer_experts(x, stats, u_bf, vt_bf, g, b):
    n = x.shape[0]
    stat_spec = pl.BlockSpec((PEER_HEADS * PEER_N_KEYS, TM_PEER), lambda i, e: (0, i))
    return pl.pallas_call(
        _peer_expert_kernel,
        grid=(n // TM_PEER, PEER_N_EXPERTS // EB_PEER),
        in_specs=[pl.BlockSpec((TM_PEER, D_MODEL), lambda i, e: (i, 0))] + [stat_spec] * 3
        + [pl.BlockSpec((EB_PEER, D_MODEL), lambda i, e: (e, 0)),
           pl.BlockSpec((D_MODEL, EB_PEER), lambda i, e: (0, e)),
           _const_spec(g.shape), _const_spec(b.shape)],
        out_specs=pl.BlockSpec((TM_PEER, D_MODEL), lambda i, e: (i, 0)),
        out_shape=jax.ShapeDtypeStruct((n, D_MODEL), F32),
        scratch_shapes=[pltpu.VMEM((TM_PEER, D_MODEL), BF16),
                        pltpu.VMEM((EB_PEER, TM_PEER), F32),
                        pltpu.VMEM((EB_PEER, TM_PEER), BF16),
                        pltpu.VMEM((D_MODEL, TM_PEER), F32)],
        compiler_params=_params("parallel", "arbitrary"),
        name="peer_experts",
    )(x, *stats, u_bf, vt_bf, g, b)


def _ple_kernel(x_ref, p_ref, wp_ref, wg_ref, y_ref):
    x = x_ref[...]
    proj = jnp.dot(p_ref[...].astype(BF16), wp_ref[...], preferred_element_type=F32)
    gate = jax.nn.sigmoid(jnp.dot(x.astype(BF16), wg_ref[...], preferred_element_type=F32))
    y_ref[...] = x + proj * gate


def _ple(x, p, wp, wg):
    n = x.shape[0]
    row = lambda i: (i, 0)
    return pl.pallas_call(
        _ple_kernel,
        grid=(n // TM,),
        in_specs=[pl.BlockSpec((TM, D_MODEL), row), pl.BlockSpec((TM, PLE_DIM), row),
                  _const_spec(wp.shape), _const_spec(wg.shape)],
        out_specs=pl.BlockSpec((TM, D_MODEL), row),
        out_shape=jax.ShapeDtypeStruct((n, D_MODEL), F32),
        compiler_params=_params("parallel"),
        name="ple_gate",
    )(x, p, wp, wg)


def _rope_partner_cols(w, head_dim, half):
    n_heads = w.shape[1] // head_dim
    w3 = w.reshape(w.shape[0], n_heads, head_dim)
    sw = jnp.concatenate([w3[..., half:2 * half], w3[..., :half],
                          jnp.zeros_like(w3[..., 2 * half:])], axis=-1)
    return sw.reshape(w.shape)


def _twice_per_head(w, head_dim):
    n_heads = w.shape[1] // head_dim
    w3 = w.reshape(w.shape[0], n_heads, head_dim)
    return jnp.concatenate([w3, w3], axis=-1).reshape(w.shape[0], 2 * w.shape[1])


def _rope_tables(seq, half, theta, lanes_before, lanes_after, pass_through, scale):
    inv = theta ** (-jnp.arange(half, dtype=F32) / half)
    ang = jnp.arange(seq).astype(F32)[:, None] * inv[None, :]
    cos, sin = jnp.cos(ang), jnp.sin(ang)
    ones = lambda w, v: jnp.full((seq, w), v, F32)
    ct = jnp.concatenate([ones(lanes_before, pass_through), cos, cos, ones(lanes_after, pass_through)], axis=-1)
    st = jnp.concatenate([ones(lanes_before, 0.0), -sin, sin, ones(lanes_after, 0.0)], axis=-1)
    return ct * scale, st * scale


def _prep_even(a_w_in, a_q_norm, a_kv_norm, a_w_uq, a_w_ukv, a_w_out):
    o1 = MLA_Q_LORA
    o2 = o1 + MLA_KV_LORA
    o3 = o2 + MLA_ROPE
    nb = DIL_HEADS * DIL_HD
    bf = lambda t: t.astype(BF16)
    w = {}
    w["wcq"] = bf(a_w_in[:, :o1])
    w["wckv"] = bf(a_w_in[:, o1:o2])
    kr = a_w_in[:, o2:o3]
    krs = jnp.concatenate([kr[:, MLA_ROPE // 2:], kr[:, :MLA_ROPE // 2]], axis=-1)
    pad_slot = lambda t: jnp.pad(t, ((0, 0), (MLA_NOPE, MLA_SLOT - MLA_NOPE - MLA_ROPE)))
    w["wkr"] = bf(jnp.concatenate([pad_slot(kr), pad_slot(krs)], axis=-1))
    qb = a_w_in[:, o3:o3 + nb]
    kb = a_w_in[:, o3 + nb:o3 + 2 * nb]
    rd2 = DIL_HD // ROPE_FRACTION // 2
    w["wqb"] = bf(jnp.concatenate([qb, _rope_partner_cols(qb, DIL_HD, rd2)], axis=-1))
    w["wkb"] = bf(jnp.concatenate([kb, _rope_partner_cols(kb, DIL_HD, rd2)], axis=-1))
    w["wvb"] = bf(a_w_in[:, o3 + 2 * nb:])
    w["qn"] = a_q_norm.reshape(1, -1)
    w["kvn"] = a_kv_norm.reshape(1, -1)
    uq = a_w_uq.reshape(MLA_Q_LORA, MLA_HEADS, MLA_NOPE + MLA_ROPE)
    zpad = jnp.zeros((MLA_Q_LORA, MLA_HEADS, MLA_SLOT - MLA_NOPE - MLA_ROPE), F32)
    uq1 = jnp.concatenate([uq, zpad], axis=-1).reshape(MLA_Q_LORA, MLA_HEADS * MLA_SLOT)
    r = uq[..., MLA_NOPE:]
    uq2 = jnp.concatenate([jnp.zeros_like(uq[..., :MLA_NOPE]), r[..., MLA_ROPE // 2:], r[..., :MLA_ROPE // 2],
                           zpad], axis=-1).reshape(MLA_Q_LORA, MLA_HEADS * MLA_SLOT)
    w["wuq"] = bf(jnp.concatenate([uq1, uq2], axis=-1))
    ukv = a_w_ukv.reshape(MLA_KV_LORA, MLA_HEADS, MLA_NOPE + MLA_V)
    kpad = jnp.zeros((MLA_KV_LORA, MLA_HEADS, MLA_SLOT - MLA_NOPE), F32)
    w["wk"] = bf(jnp.concatenate([ukv[..., :MLA_NOPE], kpad], axis=-1).reshape(MLA_KV_LORA, -1))
    w["wv"] = bf(ukv[..., MLA_NOPE:].reshape(MLA_KV_LORA, MLA_HEADS * MLA_V))
    w["wout_a"] = bf(a_w_out[:MLA_HEADS * MLA_V])
    w["wout_b"] = bf(a_w_out[MLA_HEADS * MLA_V:])
    return w


def _prep_odd(c_w_in, c_w_out):
    e1 = C_HEADS * C_HD
    e2 = e1 + C_KV_HEADS * C_HD
    rd2 = C_HD // ROPE_FRACTION // 2
    q, k = c_w_in[:, :e1], c_w_in[:, e1:e2]
    k2 = lambda t: _twice_per_head(t, C_HD)
    return {
        "wq": jnp.concatenate([q, _rope_partner_cols(q, C_HD, rd2)], axis=-1).astype(BF16),
        "wk": jnp.concatenate([k2(k), k2(_rope_partner_cols(k, C_HD, rd2))], axis=-1).astype(BF16),
        "wv": k2(c_w_in[:, e2:]).astype(BF16),
        "wout": c_w_out.astype(BF16),
    }


def _split_kernel(w_ref, hi_ref, lo_ref):
    hi_ref[...], lo_ref[...] = _split_bf16(w_ref[...])


def _split_hi_lo(w):
    flat = w.reshape(-1, LANES)
    rows = flat.shape[0]
    blk = min(rows, 1024)
    spec = pl.BlockSpec((blk, LANES), lambda i: (i, 0))
    hi, lo = pl.pallas_call(
        _split_kernel,
        grid=(rows // blk,),
        in_specs=[spec], out_specs=[spec, spec],
        out_shape=[jax.ShapeDtypeStruct(flat.shape, BF16)] * 2,
        compiler_params=_params("parallel"),
        name="split_hi_lo",
    )(flat)
    return hi.reshape(w.shape), lo.reshape(w.shape)


def _prep_peer(peer_w_q, peer_sub_keys, peer_u, peer_v):
    wq_hi, wq_lo = _split_hi_lo(jnp.swapaxes(peer_w_q, 1, 2))
    keys_hi, keys_lo = _split_hi_lo(peer_sub_keys)
    return {
        "wq_hi": wq_hi, "wq_lo": wq_lo, "keys_hi": keys_hi, "keys_lo": keys_lo,
        "u": peer_u.astype(BF16), "vt": jnp.swapaxes(peer_v, 1, 2).astype(BF16),
    }


def _tables_mla(seq):
    scale = (MLA_NOPE + MLA_ROPE) ** -0.5
    pad = MLA_SLOT - MLA_NOPE - MLA_ROPE
    cq, sq = _rope_tables(seq, MLA_ROPE // 2, MLA_THETA, MLA_NOPE, pad, 1.0, scale)
    ck, sk = _rope_tables(seq, MLA_ROPE // 2, MLA_THETA, MLA_NOPE, pad, 0.0, 1.0)
    return jnp.stack([cq, sq, ck, sk])


def _tables_partial(seq, head_dim, q_scale):
    half = head_dim // ROPE_FRACTION // 2
    cq, sq = _rope_tables(seq, half, ROPE_THETA, 0, head_dim - 2 * half, 1.0, q_scale)
    ck, sk = _rope_tables(seq, half, ROPE_THETA, 0, head_dim - 2 * half, 1.0, 1.0)
    two = lambda t: jnp.concatenate([t, t], axis=-1)
    return jnp.stack([two(cq), two(sq), two(ck), two(sk)])


def _peer_layer(x, pw, i, g, b):
    stats = _peer_select(x, pw["wq_hi"][i], pw["wq_lo"][i], pw["keys_hi"][i], pw["keys_lo"][i])
    return _peer_experts(x, stats, pw["u"][i], pw["vt"][i], g, b)


def _mixer_even(x, w, n_seq, seq, g, b):
    outs = _inproj_even(x, w, _tables_mla(seq), _tables_partial(seq, DIL_HD, DIL_HD ** -0.5), seq)
    qa, ka, va = outs[:3]
    oa = _mla_attention(qa, ka, va, n_seq, seq)
    pats = []
    for p, (window, dil) in enumerate(DIL_PATTERNS):
        qd, kd, vd = outs[3 + 3 * p:6 + 3 * p]
        pats.append(_banded_attention(qd, kd, vd, n_seq, seq, dil, (window // 2) // dil, lambda m: m, emit_lse=True))
    return _outproj_even(oa, pats, w["wout_a"], w["wout_b"], x, g, b)


def _mixer_odd(x, w, sink, n_seq, seq, g, b):
    q, k, v = _inproj_odd(x, w, _tables_partial(seq, C_HD, C_HD ** -0.5), seq)
    pairs_per_group = C_HEADS // C_KV_HEADS // 2
    (o,) = _banded_attention(q, k, v, n_seq, seq, 1, C_RADIUS, lambda m: m // pairs_per_group, sink=sink)
    return _outproj_odd(o, w["wout"], x, g, b)


def _trunk(x3, p4, prm):
    n_seq, seq, _ = x3.shape
    x = x3.reshape(n_seq * seq, D_MODEL)
    p = p4.reshape(DEPTH, n_seq * seq, PLE_DIM)
    row = lambda t: t.reshape(1, -1)
    for i in range(DEPTH):
        g, b = row(prm["ln_mix_g"][i]), row(prm["ln_mix_b"][i])
        if i % 2 == 0:
            x = _mixer_even(x, prm["even"], n_seq, seq, g, b)
        else:
            x = _mixer_odd(x, prm["odd"], prm["sink"], n_seq, seq, g, b)
        x = _peer_layer(x, prm["peer"], i, row(prm["ln_ffn_g"][i]), row(prm["ln_ffn_b"][i]))
        x = _ple(x, p[i], prm["ple_proj"][i], prm["ple_gate"][i])
    return x.reshape(n_seq, seq, D_MODEL)


def kernel(x_prompt, x_sample, p_prompt, p_sample, a_w_in, a_q_norm, a_kv_norm, a_w_uq, a_w_ukv, a_w_out,
           c_w_in, c_sink, c_w_out, ln_mix_g, ln_mix_b, ln_ffn_g, ln_ffn_b, peer_w_q, peer_sub_keys,
           peer_u, peer_v, ple_proj, ple_gate):
    prm = {
        "even": _prep_even(a_w_in[0], a_q_norm[0], a_kv_norm[0], a_w_uq[0], a_w_ukv[0], a_w_out[0]),
        "odd": _prep_odd(c_w_in[0], c_w_out[0]),
        "sink": c_sink[0],
        "ln_mix_g": ln_mix_g, "ln_mix_b": ln_mix_b, "ln_ffn_g": ln_ffn_g, "ln_ffn_b": ln_ffn_b,
        "peer": _prep_peer(peer_w_q, peer_sub_keys, peer_u, peer_v),
        "ple_proj": ple_proj.astype(BF16),
        "ple_gate": ple_gate.astype(BF16),
    }
    y_prompt = _trunk(x_prompt, p_prompt, prm)
    y_sample = _trunk(x_sample, p_sample, prm)
    return (y_prompt, y_sample)
```

```python
import functools

import jax
import jax.numpy as jnp
import numpy as np
from jax import lax
from jax.experimental import pallas as pl
from jax.experimental.pallas import tpu as pltpu

F32 = jnp.float32
BF16 = jnp.bfloat16

D_MODEL = 1024
DEPTH = 2
PLE_DIM = 256
MLA_HEADS = 8
MLA_Q_LORA = 384
MLA_KV_LORA = 256
MLA_NOPE = 64
MLA_ROPE = 32
MLA_V = 64
MLA_THETA = 10000.0
MLA_SLOT = 128
DIL_HEADS = 8
DIL_HD = 64
DIL_PATTERNS = ((128, 1), (512, 4), (2048, 16))
DIL_W = DIL_HEADS * DIL_HD
C_HEADS = 16
C_KV_HEADS = 4
C_HD = 64
C_RADIUS = 128
ROPE_THETA = 500000.0
ROPE_FRACTION = 4
PEER_HEADS = 8
PEER_N_KEYS = 128
PEER_N_EXPERTS = PEER_N_KEYS * PEER_N_KEYS
PEER_D_KEY = 128
PEER_TOPK = 16
DN_ALPHA = (2 * DEPTH) ** 0.25
LN_EPS = 1e-5
RMS_EPS = 1e-6
NEG = -1e30
INV_SQRT2 = 0.7071067811865476
NO_KEY_QUALIFIES = 1.0

LANES = 128
SUBLANES = 8
VMEM_LIMIT_BYTES = 56 * 1024 * 1024

TM = 512
TQ_MLA = 512
TK_MLA = 1024
TQ_BAND = 256
T_HALO = 128
TM_SEL = 512
TM_PEER = 512
EB_PEER = 2048

_NT = (((1,), (1,)), ((), ()))


def _params(*sem):
    return pltpu.CompilerParams(dimension_semantics=sem, vmem_limit_bytes=VMEM_LIMIT_BYTES)


def _layer_norm(y, g, b):
    mu = jnp.mean(y, axis=-1, keepdims=True)
    d = y - mu
    var = jnp.mean(d * d, axis=-1, keepdims=True)
    return d * lax.rsqrt(var + LN_EPS) * g + b


def _rms_norm(y, g):
    return y * lax.rsqrt(jnp.mean(y * y, axis=-1, keepdims=True) + RMS_EPS) * g


def _tile_lanes(t, reps):
    return jnp.concatenate([t] * reps, axis=-1)


def _tile_rows(t, reps):
    return jnp.concatenate([t] * reps, axis=0)


def _low_half_lanes():
    return lax.broadcasted_iota(jnp.int32, (1, LANES), 1) < LANES // 2


def _const_spec(shape):
    nd = len(shape)
    return pl.BlockSpec(shape, lambda *_: (0,) * nd)


def _inproj_even_kernel(x_ref, wcq_ref, wckv_ref, wkr_ref, wqb_ref, wkb_ref, wvb_ref,
                        qn_ref, kvn_ref, wuq_ref, wk_ref, wv_ref, ta_ref, tb_ref,
                        qa_ref, ka_ref, va_ref, qb_ref, kb_ref, vb_ref,
                        q4_ref, k4_ref, v4_ref, q16_ref, k16_ref, v16_ref, perm_ref):
    xb = x_ref[...].astype(BF16)
    dot = functools.partial(jnp.dot, preferred_element_type=F32)
    cq = _rms_norm(dot(xb, wcq_ref[...]), qn_ref[...]).astype(BF16)
    q12 = dot(cq, wuq_ref[...])
    cq_t = _tile_lanes(ta_ref[0], MLA_HEADS)
    sq_t = _tile_lanes(ta_ref[1], MLA_HEADS)
    qa_ref[...] = (q12[:, :1024] * cq_t + q12[:, 1024:] * sq_t).astype(BF16)
    ckv = _rms_norm(dot(xb, wckv_ref[...]), kvn_ref[...]).astype(BF16)
    kr12 = dot(xb, wkr_ref[...])
    kr = kr12[:, :LANES] * ta_ref[2] + kr12[:, LANES:] * ta_ref[3]
    ka_ref[...] = (dot(ckv, wk_ref[...]) + _tile_lanes(kr, MLA_HEADS)).astype(BF16)
    va_ref[...] = dot(ckv, wv_ref[...]).astype(BF16)
    def emit(val, nat_ref, views):
        nat_ref[...] = val.astype(BF16)
        for lt in range(DIL_W // LANES):
            perm_ref[lt] = val[:, lt * LANES:(lt + 1) * LANES]
        for d, ref in views:
            for r in range(d):
                for lt in range(DIL_W // LANES):
                    c0 = r * DIL_W + lt * LANES
                    ref[:, c0:c0 + LANES] = perm_ref[lt, pl.ds(r, TM // d, stride=d), :].astype(BF16)

    q12b = dot(xb, wqb_ref[...])
    emit(q12b[:, :512] * _tile_lanes(tb_ref[0], 4) + q12b[:, 512:] * _tile_lanes(tb_ref[1], 4),
         qb_ref, ((4, q4_ref), (16, q16_ref)))
    k12b = dot(xb, wkb_ref[...])
    emit(k12b[:, :512] * _tile_lanes(tb_ref[2], 4) + k12b[:, 512:] * _tile_lanes(tb_ref[3], 4),
         kb_ref, ((4, k4_ref), (16, k16_ref)))
    emit(dot(xb, wvb_ref[...]), vb_ref, ((4, v4_ref), (16, v16_ref)))


def _inproj_even(x, w, ta, tb, seq):
    n = x.shape[0]
    pos_blocks = seq // TM
    row = lambda i: (i, 0)
    tab = lambda i: (0, i % pos_blocks, 0)
    wnames = ("wcq", "wckv", "wkr", "wqb", "wkb", "wvb", "qn", "kvn", "wuq", "wk", "wv")
    ws = [w[k] for k in wnames]
    out_w = (1024, 1024, 512, 512, 512, 512)
    views = [4, 4, 4, 16, 16, 16]
    return pl.pallas_call(
        _inproj_even_kernel,
        grid=(n // TM,),
        in_specs=[pl.BlockSpec((TM, D_MODEL), row)] + [_const_spec(a.shape) for a in ws]
        + [pl.BlockSpec((4, TM, LANES), tab), pl.BlockSpec((4, TM, LANES), tab)],
        out_specs=[pl.BlockSpec((TM, c), row) for c in out_w]
        + [pl.BlockSpec((TM // d, d * DIL_W), row) for d in views],
        out_shape=[jax.ShapeDtypeStruct((n, c), BF16) for c in out_w]
        + [jax.ShapeDtypeStruct((n // d, d * DIL_W), BF16) for d in views],
        scratch_shapes=[pltpu.VMEM((DIL_W // LANES, TM, LANES), F32)],
        compiler_params=_params("parallel"),
        name="inproj_even",
    )(x, *ws, ta, tb)


def _mla_kernel(q_ref, k_ref, v_ref, o_ref, m_ref, l_ref, acc_ref):
    ki = pl.program_id(2)
    low = _low_half_lanes()
    reps = TK_MLA // LANES

    @pl.when(ki == 0)
    def _init():
        m_ref[...] = jnp.full(m_ref.shape, NEG, F32)
        l_ref[...] = jnp.zeros(l_ref.shape, F32)
        acc_ref[...] = jnp.zeros(acc_ref.shape, F32)

    for j in range(MLA_HEADS // 2):
        vslot = v_ref[:, j * LANES:(j + 1) * LANES]
        prod, alphas = [], []
        for half in range(2):
            h = 2 * j + half
            qh = q_ref[:, h * MLA_SLOT:(h + 1) * MLA_SLOT]
            kh = k_ref[:, h * MLA_SLOT:(h + 1) * MLA_SLOT]
            s = lax.dot_general(qh, kh, _NT, preferred_element_type=F32)
            m_prev = m_ref[h]
            m_new = jnp.maximum(m_prev, jnp.max(s, axis=-1, keepdims=True))
            alpha = jnp.exp(m_prev - m_new)
            p = jnp.exp(s - _tile_lanes(m_new, reps))
            l_ref[h] = alpha * l_ref[h] + jnp.sum(p, axis=-1, keepdims=True)
            m_ref[h] = m_new
            prod.append(jnp.dot(p.astype(BF16), vslot, preferred_element_type=F32))
            alphas.append(alpha)
        acc_ref[j] = (jnp.where(low, alphas[0], alphas[1]) * acc_ref[j]
                      + jnp.where(low, prod[0], prod[1]))

    @pl.when(ki == pl.num_programs(2) - 1)
    def _fin():
        outs = [acc_ref[j] / jnp.where(low, l_ref[2 * j], l_ref[2 * j + 1]) for j in range(MLA_HEADS // 2)]
        o_ref[...] = jnp.concatenate(outs, axis=-1).astype(BF16)


def _mla_attention(q, k, v, n_seq, seq):
    n = q.shape[0]
    nq, nk = seq // TQ_MLA, seq // TK_MLA
    return pl.pallas_call(
        _mla_kernel,
        grid=(n_seq, nq, nk),
        in_specs=[pl.BlockSpec((TQ_MLA, 1024), lambda b, i, j: (b * nq + i, 0)),
                  pl.BlockSpec((TK_MLA, 1024), lambda b, i, j: (b * nk + j, 0)),
                  pl.BlockSpec((TK_MLA, 512), lambda b, i, j: (b * nk + j, 0))],
        out_specs=pl.BlockSpec((TQ_MLA, 512), lambda b, i, j: (b * nq + i, 0)),
        out_shape=jax.ShapeDtypeStruct((n, 512), BF16),
        scratch_shapes=[pltpu.VMEM((MLA_HEADS, TQ_MLA, LANES), F32),
                        pltpu.VMEM((MLA_HEADS, TQ_MLA, LANES), F32),
                        pltpu.VMEM((MLA_HEADS // 2, TQ_MLA, LANES), F32)],
        compiler_params=_params("parallel", "parallel", "arbitrary"),
        name="mla_attention",
    )(q, k, v)


def _band_bias(radius, tq):
    row = np.arange(tq)[:, None]
    col = np.arange(tq + 2 * T_HALO)[None, :]
    d = col - T_HALO - row
    return np.where(np.abs(d) <= radius, 0.0, NEG).astype(np.float32)


def _banded_kernel(*refs, tq, n_pairs, slot_of_pair, has_sink, emit_lse):
    refs = list(refs)
    sink_ref = refs.pop(0) if has_sink else None
    q_ref, kp_ref, kc_ref, kn_ref, vp_ref, vc_ref, vn_ref, bias_ref = refs[:8]
    o_ref = refs[8]
    lse_ref = refs[9] if emit_lse else None
    qi = pl.program_id(2)
    nb = pl.num_programs(2)
    col = lax.broadcasted_iota(jnp.int32, (1, tq + 2 * T_HALO), 1)
    no_prev = jnp.where(qi == 0, NEG, 0.0)
    no_next = jnp.where(qi == nb - 1, NEG, 0.0)
    bias = (bias_ref[...] + jnp.where(col < T_HALO, no_prev, 0.0)
            + jnp.where(col >= tq + T_HALO, no_next, 0.0))
    low = _low_half_lanes()
    bias2 = _tile_rows(bias, 2)
    for m in range(n_pairs):
        sl = slice(slot_of_pair(m) * LANES, (slot_of_pair(m) + 1) * LANES)
        kcat = jnp.concatenate([kp_ref[:, sl], kc_ref[:, sl], kn_ref[:, sl]], axis=0)
        vcat = jnp.concatenate([vp_ref[:, sl], vc_ref[:, sl], vn_ref[:, sl]], axis=0)
        qp = q_ref[:, m * LANES:(m + 1) * LANES]
        zero = jnp.zeros_like(qp)
        q2 = jnp.concatenate([jnp.where(low, qp, zero), jnp.where(low, zero, qp)], axis=0)
        s = lax.dot_general(q2, kcat, _NT, preferred_element_type=F32) + bias2
        mx = jnp.max(s, axis=-1, keepdims=True)
        if has_sink:
            first = lax.broadcasted_iota(jnp.int32, (2 * tq, 1), 0) < tq
            sk = jnp.where(first, sink_ref[2 * m], sink_ref[2 * m + 1])
            mx = jnp.maximum(mx, sk)
        p = jnp.exp(s - mx)
        den = jnp.sum(p, axis=-1, keepdims=True)
        if has_sink:
            den = den + jnp.exp(sk - mx)
        res = jnp.dot(p.astype(BF16), vcat, preferred_element_type=F32) / den
        o_ref[:, m * LANES:(m + 1) * LANES] = jnp.where(low, res[:tq], res[tq:]).astype(BF16)
        if emit_lse:
            lse = mx + jnp.log(den)
            lse_ref[:, m * LANES:(m + 1) * LANES] = jnp.where(low, lse[:tq], lse[tq:])


def _banded_attention(q, k, v, n_seq, seq, dil, radius, slot_of_pair, sink=None, emit_lse=False):
    n, qw = q.shape[0] * dil, q.shape[1] // dil
    kw = k.shape[1] // dil
    sub = seq // dil
    tq = min(TQ_BAND, sub)
    nb = sub // tq
    hq = tq // T_HALO
    nh = sub // T_HALO
    cur = lambda b, r, i: (b * nb + i, r)
    prev = lambda b, r, i: (b * nh + jnp.maximum(i * hq - 1, 0), r)
    nxt = lambda b, r, i: (b * nh + jnp.minimum((i + 1) * hq, nh - 1), r)
    halo = lambda f: pl.BlockSpec((T_HALO, kw), f)
    body = pl.BlockSpec((tq, kw), cur)
    in_specs = [pl.BlockSpec((tq, qw), cur),
                halo(prev), body, halo(nxt), halo(prev), body, halo(nxt),
                pl.BlockSpec((tq, tq + 2 * T_HALO), lambda b, r, i: (0, 0))]
    args = [q, k, k, k, v, v, v, jnp.asarray(_band_bias(radius, tq))]
    if sink is not None:
        in_specs = [pl.BlockSpec(memory_space=pltpu.SMEM)] + in_specs
        args = [sink] + args
    out_specs = [pl.BlockSpec((tq, qw), cur)]
    out_shape = [jax.ShapeDtypeStruct((n // dil, dil * qw), BF16)]
    if emit_lse:
        out_specs.append(pl.BlockSpec((tq, qw), cur))
        out_shape.append(jax.ShapeDtypeStruct((n // dil, dil * qw), F32))
    return pl.pallas_call(
        functools.partial(_banded_kernel, tq=tq, n_pairs=qw // LANES, slot_of_pair=slot_of_pair,
                          has_sink=sink is not None, emit_lse=emit_lse),
        grid=(n_seq, dil, nb),
        in_specs=in_specs, out_specs=out_specs, out_shape=out_shape,
        compiler_params=_params("parallel", "parallel", "parallel"),
        name="banded_attention",
    )(*args)


def _outproj_even_kernel(oa_ref, o1_ref, l1_ref, o4_ref, l4_ref, o16_ref, l16_ref, wa_ref, wb_ref,
                         x_ref, g_ref, b_ref, y_ref, po4_ref, pl4_ref, po16_ref, pl16_ref):
    def natural(src_ref, dst_ref, d):
        for r in range(d):
            for lt in range(DIL_W // LANES):
                c0 = r * DIL_W + lt * LANES
                dst_ref[lt, pl.ds(r, TM // d, stride=d), :] = src_ref[:, c0:c0 + LANES].astype(F32)
        return jnp.concatenate([dst_ref[lt] for lt in range(DIL_W // LANES)], axis=-1)

    l1, o1 = l1_ref[...], o1_ref[...].astype(F32)
    l2, o2 = natural(l4_ref, pl4_ref, 4), natural(o4_ref, po4_ref, 4)
    l3, o3 = natural(l16_ref, pl16_ref, 16), natural(o16_ref, po16_ref, 16)
    mx = jnp.maximum(jnp.maximum(l1, l2), l3)
    e1, e2, e3 = jnp.exp(l1 - mx), jnp.exp(l2 - mx), jnp.exp(l3 - mx)
    ob = (e1 * o1 + e2 * o2 + e3 * o3) / (e1 + e2 + e3)
    mix = (jnp.dot(oa_ref[...], wa_ref[...], preferred_element_type=F32)
           + jnp.dot(ob.astype(BF16), wb_ref[...], preferred_element_type=F32))
    y_ref[...] = _layer_norm(DN_ALPHA * x_ref[...] + mix, g_ref[...], b_ref[...])


def _outproj_even(oa, pats, wa, wb, x, g, b):
    n = x.shape[0]
    row = lambda i: (i, 0)
    pat_specs, pat_args = [], []
    for (_, d), (o, lse) in zip(DIL_PATTERNS, pats):
        pat_specs += [pl.BlockSpec((TM // d, d * DIL_W), row)] * 2
        pat_args += [o, lse]
    return pl.pallas_call(
        _outproj_even_kernel,
        grid=(n // TM,),
        in_specs=[pl.BlockSpec((TM, DIL_W), row)] + pat_specs
        + [_const_spec(wa.shape), _const_spec(wb.shape),
           pl.BlockSpec((TM, D_MODEL), row), _const_spec(g.shape), _const_spec(b.shape)],
        out_specs=pl.BlockSpec((TM, D_MODEL), row),
        out_shape=jax.ShapeDtypeStruct((n, D_MODEL), F32),
        scratch_shapes=[pltpu.VMEM((DIL_W // LANES, TM, LANES), F32)] * 4,
        compiler_params=_params("parallel"),
        name="outproj_even",
    )(oa, *pat_args, wa, wb, x, g, b)


def _outproj_odd_kernel(o_ref, w_ref, x_ref, g_ref, b_ref, y_ref):
    mix = jnp.dot(o_ref[...], w_ref[...], preferred_element_type=F32)
    y_ref[...] = _layer_norm(DN_ALPHA * x_ref[...] + mix, g_ref[...], b_ref[...])


def _outproj_odd(o, w, x, g, b):
    n = x.shape[0]
    row = lambda i: (i, 0)
    return pl.pallas_call(
        _outproj_odd_kernel,
        grid=(n // TM,),
        in_specs=[pl.BlockSpec((TM, o.shape[1]), row), _const_spec(w.shape),
                  pl.BlockSpec((TM, D_MODEL), row), _const_spec(g.shape), _const_spec(b.shape)],
        out_specs=pl.BlockSpec((TM, D_MODEL), row),
        out_shape=jax.ShapeDtypeStruct((n, D_MODEL), F32),
        compiler_params=_params("parallel"),
        name="outproj_odd",
    )(o, w, x, g, b)


def _inproj_odd_kernel(x_ref, wq_ref, wk_ref, wv_ref, tb_ref, q_ref, k_ref, v_ref):
    xb = x_ref[...].astype(BF16)
    dot = functools.partial(jnp.dot, preferred_element_type=F32)
    q12 = dot(xb, wq_ref[...])
    q_ref[...] = (q12[:, :1024] * _tile_lanes(tb_ref[0], 8)
                  + q12[:, 1024:] * _tile_lanes(tb_ref[1], 8)).astype(BF16)
    k12 = dot(xb, wk_ref[...])
    k_ref[...] = (k12[:, :512] * _tile_lanes(tb_ref[2], 4)
                  + k12[:, 512:] * _tile_lanes(tb_ref[3], 4)).astype(BF16)
    v_ref[...] = dot(xb, wv_ref[...]).astype(BF16)


def _inproj_odd(x, w, tb, seq):
    n = x.shape[0]
    pos_blocks = seq // TM
    row = lambda i: (i, 0)
    ws = [w["wq"], w["wk"], w["wv"]]
    out_w = (1024, 512, 512)
    return pl.pallas_call(
        _inproj_odd_kernel,
        grid=(n // TM,),
        in_specs=[pl.BlockSpec((TM, D_MODEL), row)] + [_const_spec(a.shape) for a in ws]
        + [pl.BlockSpec((4, TM, LANES), lambda i: (0, i % pos_blocks, 0))],
        out_specs=[pl.BlockSpec((TM, c), row) for c in out_w],
        out_shape=[jax.ShapeDtypeStruct((n, c), BF16) for c in out_w],
        compiler_params=_params("parallel"),
        name="inproj_odd",
    )(x, *ws, tb)


def _sorting_network(n):
    pairs = []

    def merge(lo, hi, r):
        step = r * 2
        if step < hi - lo:
            merge(lo, hi, step)
            merge(lo + r, hi, step)
            pairs.extend((i, i + r) for i in range(lo + r, hi - r, step))
        else:
            pairs.append((lo, lo + r))

    def sort(lo, hi):
        if hi - lo >= 1:
            mid = lo + (hi - lo) // 2
            sort(lo, mid)
            sort(mid + 1, hi)
            merge(lo, hi, 1)

    sort(0, n - 1)
    return pairs


def _top16_sorted(s):
    k = PEER_TOPK
    w = [s[g * SUBLANES:(g + 1) * SUBLANES, :] for g in range(s.shape[0] // SUBLANES)]
    w += [jnp.full_like(w[0], NEG)] * (k - len(w))

    def exchange(i, j):
        w[i], w[j] = jnp.maximum(w[i], w[j]), jnp.minimum(w[i], w[j])

    for i, j in _sorting_network(k):
        exchange(i, j)
    shift = SUBLANES // 2
    while shift >= 1:
        other = [pltpu.roll(t, shift, 0) for t in w]
        w = [jnp.maximum(w[p], other[k - 1 - p]) for p in range(k)]
        d = k // 2
        while d >= 1:
            for i in range(k):
                if i & d == 0:
                    exchange(i, i + d)
            d //= 2
        shift //= 2
    return w


def _split_bf16(t):
    hi = t.astype(BF16)
    return hi, (t - hi.astype(F32)).astype(BF16)


def _dot3(a_hi, a_lo, b_hi, b_lo, dims):
    d = functools.partial(lax.dot_general, dimension_numbers=dims, preferred_element_type=F32)
    return d(a_hi, b_hi) + (d(a_hi, b_lo) + d(a_lo, b_hi))


def _peer_select_kernel(x_ref, wqh_ref, wql_ref, kh_ref, kl_ref, e2_ref, thr_ref, e1_ref,
                        qh_ref, ql_ref, a_ref, b_ref):
    h = pl.program_id(1)

    @pl.when(h == 0)
    def _project():
        xh, xl = _split_bf16(x_ref[...])
        q = _dot3(wqh_ref[...], wql_ref[...], xh, xl, _NT)
        qh_ref[...], ql_ref[...] = _split_bf16(q)

    half = PEER_D_KEY // 2
    row0 = pl.multiple_of(h * PEER_D_KEY, PEER_D_KEY)
    mm = (((1,), (0,)), ((), ()))
    s1 = _dot3(kh_ref[0, 0], kl_ref[0, 0], qh_ref[pl.ds(row0, half), :], ql_ref[pl.ds(row0, half), :], mm)
    s2 = _dot3(kh_ref[0, 1], kl_ref[0, 1], qh_ref[pl.ds(row0 + half, half), :],
               ql_ref[pl.ds(row0 + half, half), :], mm)
    for lt in range(TM_SEL // LANES):
        ls = slice(lt * LANES, (lt + 1) * LANES)
        s1t, s2t = s1[:, ls], s2[:, ls]
        a_rep, b_rep = _top16_sorted(s1t), _top16_sorted(s2t)
        for p in range(PEER_TOPK):
            a_ref[p:p + 1, ls] = a_rep[p][0:1]
            b_ref[p:p + 1, ls] = b_rep[p][0:1]
        a16, b16 = a_ref[:, ls], b_ref[:, ls]
        a0, b0 = a16[0:1], b16[0:1]
        cand = [a0 + b16]
        cand += [a16[i:i + 1] + b16[0:8] for i in range(1, 8)]
        cand += [a16[8:16] + b0]
        cand = jnp.concatenate(cand, axis=0)
        v16 = _top16_sorted(cand)[PEER_TOPK - 1][0:1]
        z = jnp.sum(jnp.where(cand >= v16, jnp.exp(cand - (a0 + b0)), 0.0), axis=0, keepdims=True)
        half_inv_z = 0.5 / z
        e2_sorted = jnp.exp(b16 - b0) * half_inv_z
        thr = jnp.full(s1t.shape, NO_KEY_QUALIFIES, F32)
        for i in range(PEER_TOPK):
            t_i = jnp.min(jnp.where(a16[i:i + 1] + b16 >= v16, e2_sorted, NO_KEY_QUALIFIES), axis=0, keepdims=True)
            thr = jnp.where(s1t == _tile_rows(a_rep[i], PEER_N_KEYS // SUBLANES), t_i, thr)
        e2_ref[:, ls] = jnp.exp(s2t - b0) * half_inv_z
        thr_ref[:, ls] = thr
        e1_ref[:, ls] = jnp.exp(s1t - a0)


def _peer_select(x, wq_hi, wq_lo, keys_hi, keys_lo):
    n = x.shape[0]
    spec = pl.BlockSpec((PEER_N_KEYS, TM_SEL), lambda i, h: (h, i))
    shape = lambda dt: jax.ShapeDtypeStruct((PEER_HEADS * PEER_N_KEYS, n), dt)
    kspec = pl.BlockSpec((1, 2, PEER_N_KEYS, PEER_D_KEY // 2), lambda i, h: (h, 0, 0, 0))
    return pl.pallas_call(
        _peer_select_kernel,
        grid=(n // TM_SEL, PEER_HEADS),
        in_specs=[pl.BlockSpec((TM_SEL, D_MODEL), lambda i, h: (i, 0)),
                  _const_spec(wq_hi.shape), _const_spec(wq_lo.shape), kspec, kspec],
        out_specs=[spec] * 3,
        out_shape=[shape(F32)] * 3,
        scratch_shapes=[pltpu.VMEM((PEER_HEADS * PEER_D_KEY, TM_SEL), BF16),
                        pltpu.VMEM((PEER_HEADS * PEER_D_KEY, TM_SEL), BF16),
                        pltpu.VMEM((PEER_TOPK, TM_SEL), F32),
                        pltpu.VMEM((PEER_TOPK, TM_SEL), F32)],
        compiler_params=_params("parallel", "arbitrary"),
        name="peer_select",
    )(x, wq_hi, wq_lo, keys_hi, keys_lo)


def _peer_expert_kernel(x_ref, e2_ref, thr_ref, e1_ref, u_ref, vt_ref, g_ref, b_ref, y_ref,
                        xb_ref, a_ref, h_ref, acc_ref):
    eb = pl.program_id(1)
    n_i1 = EB_PEER // PEER_N_KEYS
    halves = 2
    rows_half = EB_PEER // halves
    n_r = 2
    slab = PEER_N_KEYS // 2

    @pl.when(eb == 0)
    def _init():
        xb_ref[...] = x_ref[...].astype(BF16)
        acc_ref[...] = jnp.zeros(acc_ref.shape, F32)

    for hh in range(halves):
        rs = slice(hh * rows_half, (hh + 1) * rows_half)
        a_ref[rs, :] = lax.dot_general(u_ref[rs, :], xb_ref[...], _NT, preferred_element_type=F32)
    i1_base = pl.multiple_of(eb * n_i1, SUBLANES)
    for hh in range(halves):
        for lt in range(TM_PEER // LANES):
            ls = slice(lt * LANES, (lt + 1) * LANES)
            for rp in range(n_i1 // halves // n_r):
                r0 = hh * (n_i1 // halves) + n_r * rp
                for sb in range(PEER_N_KEYS // slab):
                    gate = [jnp.zeros((slab, LANES), F32)] * n_r
                    for h in range(PEER_HEADS):
                        t8 = thr_ref[pl.ds(h * PEER_N_KEYS + i1_base, n_i1), ls]
                        e8 = e1_ref[pl.ds(h * PEER_N_KEYS + i1_base, n_i1), ls]
                        e2t = e2_ref[h * PEER_N_KEYS + sb * slab:h * PEER_N_KEYS + (sb + 1) * slab, ls]
                        for t in range(n_r):
                            r = r0 + t
                            gate[t] = gate[t] + jnp.where(e2t >= t8[r:r + 1], e2t, 0.0) * e8[r:r + 1]
                    for t in range(n_r):
                        rows = slice((r0 + t) * PEER_N_KEYS + sb * slab, (r0 + t) * PEER_N_KEYS + (sb + 1) * slab)
                        a = a_ref[rows, ls]
                        h_ref[rows, ls] = (a * (1.0 + lax.erf(a * INV_SQRT2)) * gate[t]).astype(BF16)
        rs = slice(hh * rows_half, (hh + 1) * rows_half)
        acc_ref[...] += jnp.dot(vt_ref[:, rs], h_ref[rs, :], preferred_element_type=F32)

    @pl.when(eb == pl.num_programs(1) - 1)
    def _fin():
        y = DN_ALPHA * x_ref[...] + acc_ref[...].T
        y_ref[...] = _layer_norm(y, g_ref[...], b_ref[...])


def _peer_experts(x, stats, u_bf, vt_bf, layer, g, b):
    n = x.shape[0]
    stat_spec = pl.BlockSpec((PEER_HEADS * PEER_N_KEYS, TM_PEER), lambda i, e: (0, i))
    return pl.pallas_call(
        _peer_expert_kernel,
        grid=(n // TM_PEER, PEER_N_EXPERTS // EB_PEER),
        in_specs=[pl.BlockSpec((TM_PEER, D_MODEL), lambda i, e: (i, 0))] + [stat_spec] * 3
        + [pl.BlockSpec((None, EB_PEER, D_MODEL), lambda i, e: (layer, e, 0)),
           pl.BlockSpec((None, D_MODEL, EB_PEER), lambda i, e: (layer, 0, e)),
           _const_spec(g.shape), _const_spec(b.shape)],
        out_specs=pl.BlockSpec((TM_PEER, D_MODEL), lambda i, e: (i, 0)),
        out_shape=jax.ShapeDtypeStruct((n, D_MODEL), F32),
        scratch_shapes=[pltpu.VMEM((TM_PEER, D_MODEL), BF16),
                        pltpu.VMEM((EB_PEER, TM_PEER), F32),
                        pltpu.VMEM((EB_PEER, TM_PEER), BF16),
                        pltpu.VMEM((D_MODEL, TM_PEER), F32)],
        compiler_params=_params("parallel", "arbitrary"),
        name="peer_experts",
    )(x, *stats, u_bf, vt_bf, g, b)


def _ple_kernel(x_ref, p_ref, wp_ref, wg_ref, y_ref):
    x = x_ref[...]
    proj = jnp.dot(p_ref[...].astype(BF16), wp_ref[...], preferred_element_type=F32)
    gate = jax.nn.sigmoid(jnp.dot(x.astype(BF16), wg_ref[...], preferred_element_type=F32))
    y_ref[...] = x + proj * gate


def _ple(x, p, layer, wp, wg):
    n = x.shape[0]
    row = lambda i: (i, 0)
    return pl.pallas_call(
        _ple_kernel,
        grid=(n // TM,),
        in_specs=[pl.BlockSpec((TM, D_MODEL), row), pl.BlockSpec((None, TM, PLE_DIM), lambda i: (layer, i, 0)),
                  _const_spec(wp.shape), _const_spec(wg.shape)],
        out_specs=pl.BlockSpec((TM, D_MODEL), row),
        out_shape=jax.ShapeDtypeStruct((n, D_MODEL), F32),
        compiler_params=_params("parallel"),
        name="ple_gate",
    )(x, p, wp, wg)


def _rope_partner_cols(w, head_dim, half):
    n_heads = w.shape[1] // head_dim
    w3 = w.reshape(w.shape[0], n_heads, head_dim)
    sw = jnp.concatenate([w3[..., half:2 * half], w3[..., :half],
                          jnp.zeros_like(w3[..., 2 * half:])], axis=-1)
    return sw.reshape(w.shape)


def _twice_per_head(w, head_dim):
    n_heads = w.shape[1] // head_dim
    w3 = w.reshape(w.shape[0], n_heads, head_dim)
    return jnp.concatenate([w3, w3], axis=-1).reshape(w.shape[0], 2 * w.shape[1])


def _rope_tables(seq, half, theta, lanes_before, lanes_after, pass_through, scale):
    inv = theta ** (-jnp.arange(half, dtype=F32) / half)
    ang = jnp.arange(seq).astype(F32)[:, None] * inv[None, :]
    cos, sin = jnp.cos(ang), jnp.sin(ang)
    ones = lambda w, v: jnp.full((seq, w), v, F32)
    ct = jnp.concatenate([ones(lanes_before, pass_through), cos, cos, ones(lanes_after, pass_through)], axis=-1)
    st = jnp.concatenate([ones(lanes_before, 0.0), -sin, sin, ones(lanes_after, 0.0)], axis=-1)
    return ct * scale, st * scale


def _prep_even(a_w_in, a_q_norm, a_kv_norm, a_w_uq, a_w_ukv, a_w_out):
    o1 = MLA_Q_LORA
    o2 = o1 + MLA_KV_LORA
    o3 = o2 + MLA_ROPE
    nb = DIL_HEADS * DIL_HD
    bf = lambda t: t.astype(BF16)
    w = {}
    w["wcq"] = bf(a_w_in[:, :o1])
    w["wckv"] = bf(a_w_in[:, o1:o2])
    kr = a_w_in[:, o2:o3]
    krs = jnp.concatenate([kr[:, MLA_ROPE // 2:], kr[:, :MLA_ROPE // 2]], axis=-1)
    pad_slot = lambda t: jnp.pad(t, ((0, 0), (MLA_NOPE, MLA_SLOT - MLA_NOPE - MLA_ROPE)))
    w["wkr"] = bf(jnp.concatenate([pad_slot(kr), pad_slot(krs)], axis=-1))
    qb = a_w_in[:, o3:o3 + nb]
    kb = a_w_in[:, o3 + nb:o3 + 2 * nb]
    rd2 = DIL_HD // ROPE_FRACTION // 2
    w["wqb"] = bf(jnp.concatenate([qb, _rope_partner_cols(qb, DIL_HD, rd2)], axis=-1))
    w["wkb"] = bf(jnp.concatenate([kb, _rope_partner_cols(kb, DIL_HD, rd2)], axis=-1))
    w["wvb"] = bf(a_w_in[:, o3 + 2 * nb:])
    w["qn"] = a_q_norm.reshape(1, -1)
    w["kvn"] = a_kv_norm.reshape(1, -1)
    uq = a_w_uq.reshape(MLA_Q_LORA, MLA_HEADS, MLA_NOPE + MLA_ROPE)
    zpad = jnp.zeros((MLA_Q_LORA, MLA_HEADS, MLA_SLOT - MLA_NOPE - MLA_ROPE), F32)
    uq1 = jnp.concatenate([uq, zpad], axis=-1).reshape(MLA_Q_LORA, MLA_HEADS * MLA_SLOT)
    r = uq[..., MLA_NOPE:]
    uq2 = jnp.concatenate([jnp.zeros_like(uq[..., :MLA_NOPE]), r[..., MLA_ROPE // 2:], r[..., :MLA_ROPE // 2],
                           zpad], axis=-1).reshape(MLA_Q_LORA, MLA_HEADS * MLA_SLOT)
    w["wuq"] = bf(jnp.concatenate([uq1, uq2], axis=-1))
    ukv = a_w_ukv.reshape(MLA_KV_LORA, MLA_HEADS, MLA_NOPE + MLA_V)
    kpad = jnp.zeros((MLA_KV_LORA, MLA_HEADS, MLA_SLOT - MLA_NOPE), F32)
    w["wk"] = bf(jnp.concatenate([ukv[..., :MLA_NOPE], kpad], axis=-1).reshape(MLA_KV_LORA, -1))
    w["wv"] = bf(ukv[..., MLA_NOPE:].reshape(MLA_KV_LORA, MLA_HEADS * MLA_V))
    w["wout_a"] = bf(a_w_out[:MLA_HEADS * MLA_V])
    w["wout_b"] = bf(a_w_out[MLA_HEADS * MLA_V:])
    return w


def _prep_odd(c_w_in, c_w_out):
    e1 = C_HEADS * C_HD
    e2 = e1 + C_KV_HEADS * C_HD
    rd2 = C_HD // ROPE_FRACTION // 2
    q, k = c_w_in[:, :e1], c_w_in[:, e1:e2]
    k2 = lambda t: _twice_per_head(t, C_HD)
    return {
        "wq": jnp.concatenate([q, _rope_partner_cols(q, C_HD, rd2)], axis=-1).astype(BF16),
        "wk": jnp.concatenate([k2(k), k2(_rope_partner_cols(k, C_HD, rd2))], axis=-1).astype(BF16),
        "wv": k2(c_w_in[:, e2:]).astype(BF16),
        "wout": c_w_out.astype(BF16),
    }


def _split_kernel(w_ref, hi_ref, lo_ref):
    hi_ref[...], lo_ref[...] = _split_bf16(w_ref[...])


def _split_hi_lo(w):
    flat = w.reshape(-1, LANES)
    rows = flat.shape[0]
    blk = min(rows, 1024)
    spec = pl.BlockSpec((blk, LANES), lambda i: (i, 0))
    hi, lo = pl.pallas_call(
        _split_kernel,
        grid=(rows // blk,),
        in_specs=[spec], out_specs=[spec, spec],
        out_shape=[jax.ShapeDtypeStruct(flat.shape, BF16)] * 2,
        compiler_params=_params("parallel"),
        name="split_hi_lo",
    )(flat)
    return hi.reshape(w.shape), lo.reshape(w.shape)


def _prep_peer(peer_w_q, peer_sub_keys, peer_u, peer_v):
    wq_hi, wq_lo = _split_hi_lo(jnp.swapaxes(peer_w_q, 1, 2))
    keys_hi, keys_lo = _split_hi_lo(peer_sub_keys)
    return {
        "wq_hi": wq_hi, "wq_lo": wq_lo, "keys_hi": keys_hi, "keys_lo": keys_lo,
        "u": peer_u.astype(BF16), "vt": jnp.swapaxes(peer_v, 1, 2).astype(BF16),
    }


def _tables_mla(seq):
    scale = (MLA_NOPE + MLA_ROPE) ** -0.5
    pad = MLA_SLOT - MLA_NOPE - MLA_ROPE
    cq, sq = _rope_tables(seq, MLA_ROPE // 2, MLA_THETA, MLA_NOPE, pad, 1.0, scale)
    ck, sk = _rope_tables(seq, MLA_ROPE // 2, MLA_THETA, MLA_NOPE, pad, 0.0, 1.0)
    return jnp.stack([cq, sq, ck, sk])


def _tables_partial(seq, head_dim, q_scale):
    half = head_dim // ROPE_FRACTION // 2
    cq, sq = _rope_tables(seq, half, ROPE_THETA, 0, head_dim - 2 * half, 1.0, q_scale)
    ck, sk = _rope_tables(seq, half, ROPE_THETA, 0, head_dim - 2 * half, 1.0, 1.0)
    two = lambda t: jnp.concatenate([t, t], axis=-1)
    return jnp.stack([two(cq), two(sq), two(ck), two(sk)])


def _peer_layer(x, pw, i, g, b):
    stats = _peer_select(x, pw["wq_hi"][i], pw["wq_lo"][i], pw["keys_hi"][i], pw["keys_lo"][i])
    return _peer_experts(x, stats, pw["u"], pw["vt"], i, g, b)


def _mixer_even(x, w, n_seq, seq, g, b):
    outs = _inproj_even(x, w, _tables_mla(seq), _tables_partial(seq, DIL_HD, DIL_HD ** -0.5), seq)
    qa, ka, va = outs[:3]
    oa = _mla_attention(qa, ka, va, n_seq, seq)
    pats = []
    for p, (window, dil) in enumerate(DIL_PATTERNS):
        qd, kd, vd = outs[3 + 3 * p:6 + 3 * p]
        pats.append(_banded_attention(qd, kd, vd, n_seq, seq, dil, (window // 2) // dil, lambda m: m, emit_lse=True))
    return _outproj_even(oa, pats, w["wout_a"], w["wout_b"], x, g, b)


def _mixer_odd(x, w, sink, n_seq, seq, g, b):
    q, k, v = _inproj_odd(x, w, _tables_partial(seq, C_HD, C_HD ** -0.5), seq)
    pairs_per_group = C_HEADS // C_KV_HEADS // 2
    (o,) = _banded_attention(q, k, v, n_seq, seq, 1, C_RADIUS, lambda m: m // pairs_per_group, sink=sink)
    return _outproj_odd(o, w["wout"], x, g, b)


def _trunk(x3, p4, prm):
    n_seq, seq, _ = x3.shape
    x = x3.reshape(n_seq * seq, D_MODEL)
    p = p4.reshape(DEPTH, n_seq * seq, PLE_DIM)
    row = lambda t: t.reshape(1, -1)
    for i in range(DEPTH):
        g, b = row(prm["ln_mix_g"][i]), row(prm["ln_mix_b"][i])
        if i % 2 == 0:
            x = _mixer_even(x, prm["even"], n_seq, seq, g, b)
        else:
            x = _mixer_odd(x, prm["odd"], prm["sink"], n_seq, seq, g, b)
        x = _peer_layer(x, prm["peer"], i, row(prm["ln_ffn_g"][i]), row(prm["ln_ffn_b"][i]))
        x = _ple(x, p, i, prm["ple_proj"][i], prm["ple_gate"][i])
    return x.reshape(n_seq, seq, D_MODEL)


def kernel(x_prompt, x_sample, p_prompt, p_sample, a_w_in, a_q_norm, a_kv_norm, a_w_uq, a_w_ukv, a_w_out,
           c_w_in, c_sink, c_w_out, ln_mix_g, ln_mix_b, ln_ffn_g, ln_ffn_b, peer_w_q, peer_sub_keys,
           peer_u, peer_v, ple_proj, ple_gate):
    prm = {
        "even": _prep_even(a_w_in[0], a_q_norm[0], a_kv_norm[0], a_w_uq[0], a_w_ukv[0], a_w_out[0]),
        "odd": _prep_odd(c_w_in[0], c_w_out[0]),
        "sink": c_sink[0],
        "ln_mix_g": ln_mix_g, "ln_mix_b": ln_mix_b, "ln_ffn_g": ln_ffn_g, "ln_ffn_b": ln_ffn_b,
        "peer": _prep_peer(peer_w_q, peer_sub_keys, peer_u, peer_v),
        "ple_proj": ple_proj.astype(BF16),
        "ple_gate": ple_gate.astype(BF16),
    }
    y_prompt = _trunk(x_prompt, p_prompt, prm)
    y_sample = _trunk(x_sample, p_sample, prm)
    return (y_prompt, y_sample)
```

```python
import functools

import jax
import jax.numpy as jnp
import numpy as np
from jax import lax
from jax.experimental import pallas as pl
from jax.experimental.pallas import tpu as pltpu

F32 = jnp.float32
BF16 = jnp.bfloat16

D_MODEL = 1024
DEPTH = 2
PLE_DIM = 256
MLA_HEADS = 8
MLA_Q_LORA = 384
MLA_KV_LORA = 256
MLA_NOPE = 64
MLA_ROPE = 32
MLA_V = 64
MLA_THETA = 10000.0
MLA_SLOT = 128
DIL_HEADS = 8
DIL_HD = 64
DIL_PATTERNS = ((128, 1), (512, 4), (2048, 16))
DIL_W = DIL_HEADS * DIL_HD
C_HEADS = 16
C_KV_HEADS = 4
C_HD = 64
C_RADIUS = 128
ROPE_THETA = 500000.0
ROPE_FRACTION = 4
PEER_HEADS = 8
PEER_N_KEYS = 128
PEER_N_EXPERTS = PEER_N_KEYS * PEER_N_KEYS
PEER_D_KEY = 128
PEER_TOPK = 16
DN_ALPHA = (2 * DEPTH) ** 0.25
LN_EPS = 1e-5
RMS_EPS = 1e-6
NEG = -1e30
INV_SQRT2 = 0.7071067811865476
NO_KEY_QUALIFIES = 1.0

LANES = 128
SUBLANES = 8
VMEM_LIMIT_BYTES = 56 * 1024 * 1024

TM = 512
TQ_MLA = 512
TK_MLA = 2048
TQ_BAND = 256
T_HALO = 128
TM_SEL = 1024
TM_PEER = 512
EB_PEER = 2048

_NT = (((1,), (1,)), ((), ()))


def _params(*sem):
    return pltpu.CompilerParams(dimension_semantics=sem, vmem_limit_bytes=VMEM_LIMIT_BYTES)


def _layer_norm(y, g, b):
    mu = jnp.mean(y, axis=-1, keepdims=True)
    d = y - mu
    var = jnp.mean(d * d, axis=-1, keepdims=True)
    return d * lax.rsqrt(var + LN_EPS) * g + b


def _rms_norm(y, g):
    return y * lax.rsqrt(jnp.mean(y * y, axis=-1, keepdims=True) + RMS_EPS) * g


def _tile_lanes(t, reps):
    return jnp.concatenate([t] * reps, axis=-1)


def _tile_rows(t, reps):
    return jnp.concatenate([t] * reps, axis=0)


def _low_half_lanes():
    return lax.broadcasted_iota(jnp.int32, (1, LANES), 1) < LANES // 2


def _const_spec(shape):
    nd = len(shape)
    return pl.BlockSpec(shape, lambda *_: (0,) * nd)


def _inproj_even_kernel(x_ref, wcq_ref, wckv_ref, wkr_ref, wqb_ref, wkb_ref, wvb_ref,
                        qn_ref, kvn_ref, wuq_ref, wk_ref, wv_ref, ta_ref, tb_ref,
                        qa_ref, ka_ref, va_ref, qb_ref, kb_ref, vb_ref,
                        q4_ref, k4_ref, v4_ref, q16_ref, k16_ref, v16_ref, perm_ref):
    xb = x_ref[...].astype(BF16)
    dot = functools.partial(jnp.dot, preferred_element_type=F32)
    cq = _rms_norm(dot(xb, wcq_ref[...]), qn_ref[...]).astype(BF16)
    q12 = dot(cq, wuq_ref[...])
    cq_t = _tile_lanes(ta_ref[0], MLA_HEADS)
    sq_t = _tile_lanes(ta_ref[1], MLA_HEADS)
    qa_ref[...] = (q12[:, :1024] * cq_t + q12[:, 1024:] * sq_t).astype(BF16)
    ckv = _rms_norm(dot(xb, wckv_ref[...]), kvn_ref[...]).astype(BF16)
    kr12 = dot(xb, wkr_ref[...])
    kr = kr12[:, :LANES] * ta_ref[2] + kr12[:, LANES:] * ta_ref[3]
    ka_ref[...] = (dot(ckv, wk_ref[...]) + _tile_lanes(kr, MLA_HEADS)).astype(BF16)
    va_ref[...] = dot(ckv, wv_ref[...]).astype(BF16)
    def emit(val, nat_ref, views):
        nat_ref[...] = val.astype(BF16)
        for lt in range(DIL_W // LANES):
            perm_ref[lt] = val[:, lt * LANES:(lt + 1) * LANES]
        for d, ref in views:
            for r in range(d):
                for lt in range(DIL_W // LANES):
                    c0 = r * DIL_W + lt * LANES
                    ref[:, c0:c0 + LANES] = perm_ref[lt, pl.ds(r, TM // d, stride=d), :].astype(BF16)

    q12b = dot(xb, wqb_ref[...])
    emit(q12b[:, :512] * _tile_lanes(tb_ref[0], 4) + q12b[:, 512:] * _tile_lanes(tb_ref[1], 4),
         qb_ref, ((4, q4_ref), (16, q16_ref)))
    k12b = dot(xb, wkb_ref[...])
    emit(k12b[:, :512] * _tile_lanes(tb_ref[2], 4) + k12b[:, 512:] * _tile_lanes(tb_ref[3], 4),
         kb_ref, ((4, k4_ref), (16, k16_ref)))
    emit(dot(xb, wvb_ref[...]), vb_ref, ((4, v4_ref), (16, v16_ref)))


def _inproj_even(x, w, ta, tb, seq):
    n = x.shape[0]
    pos_blocks = seq // TM
    row = lambda i: (i, 0)
    tab = lambda i: (0, i % pos_blocks, 0)
    wnames = ("wcq", "wckv", "wkr", "wqb", "wkb", "wvb", "qn", "kvn", "wuq", "wk", "wv")
    ws = [w[k] for k in wnames]
    out_w = (1024, 1024, 512, 512, 512, 512)
    views = [4, 4, 4, 16, 16, 16]
    return pl.pallas_call(
        _inproj_even_kernel,
        grid=(n // TM,),
        in_specs=[pl.BlockSpec((TM, D_MODEL), row)] + [_const_spec(a.shape) for a in ws]
        + [pl.BlockSpec((4, TM, LANES), tab), pl.BlockSpec((4, TM, LANES), tab)],
        out_specs=[pl.BlockSpec((TM, c), row) for c in out_w]
        + [pl.BlockSpec((TM // d, d * DIL_W), row) for d in views],
        out_shape=[jax.ShapeDtypeStruct((n, c), BF16) for c in out_w]
        + [jax.ShapeDtypeStruct((n // d, d * DIL_W), BF16) for d in views],
        scratch_shapes=[pltpu.VMEM((DIL_W // LANES, TM, LANES), F32)],
        compiler_params=_params("parallel"),
        name="inproj_even",
    )(x, *ws, ta, tb)


def _mla_kernel(q_ref, k_ref, v_ref, o_ref, m_ref, l_ref, acc_ref):
    ki = pl.program_id(2)
    low = _low_half_lanes()
    reps = TK_MLA // LANES

    @pl.when(ki == 0)
    def _init():
        m_ref[...] = jnp.full(m_ref.shape, NEG, F32)
        l_ref[...] = jnp.zeros(l_ref.shape, F32)
        acc_ref[...] = jnp.zeros(acc_ref.shape, F32)

    for j in range(MLA_HEADS // 2):
        vslot = v_ref[:, j * LANES:(j + 1) * LANES]
        prod, alphas = [], []
        for half in range(2):
            h = 2 * j + half
            qh = q_ref[:, h * MLA_SLOT:(h + 1) * MLA_SLOT]
            kh = k_ref[:, h * MLA_SLOT:(h + 1) * MLA_SLOT]
            s = lax.dot_general(qh, kh, _NT, preferred_element_type=F32)
            m_prev = m_ref[h]
            m_new = jnp.maximum(m_prev, jnp.max(s, axis=-1, keepdims=True))
            alpha = jnp.exp(m_prev - m_new)
            p = jnp.exp(s - _tile_lanes(m_new, reps))
            l_ref[h] = alpha * l_ref[h] + jnp.sum(p, axis=-1, keepdims=True)
            m_ref[h] = m_new
            prod.append(jnp.dot(p.astype(BF16), vslot, preferred_element_type=F32))
            alphas.append(alpha)
        acc_ref[j] = (jnp.where(low, alphas[0], alphas[1]) * acc_ref[j]
                      + jnp.where(low, prod[0], prod[1]))

    @pl.when(ki == pl.num_programs(2) - 1)
    def _fin():
        outs = [acc_ref[j] / jnp.where(low, l_ref[2 * j], l_ref[2 * j + 1]) for j in range(MLA_HEADS // 2)]
        o_ref[...] = jnp.concatenate(outs, axis=-1).astype(BF16)


def _mla_attention(q, k, v, n_seq, seq):
    n = q.shape[0]
    nq, nk = seq // TQ_MLA, seq // TK_MLA
    return pl.pallas_call(
        _mla_kernel,
        grid=(n_seq, nq, nk),
        in_specs=[pl.BlockSpec((TQ_MLA, 1024), lambda b, i, j: (b * nq + i, 0)),
                  pl.BlockSpec((TK_MLA, 1024), lambda b, i, j: (b * nk + j, 0)),
                  pl.BlockSpec((TK_MLA, 512), lambda b, i, j: (b * nk + j, 0))],
        out_specs=pl.BlockSpec((TQ_MLA, 512), lambda b, i, j: (b * nq + i, 0)),
        out_shape=jax.ShapeDtypeStruct((n, 512), BF16),
        scratch_shapes=[pltpu.VMEM((MLA_HEADS, TQ_MLA, LANES), F32),
                        pltpu.VMEM((MLA_HEADS, TQ_MLA, LANES), F32),
                        pltpu.VMEM((MLA_HEADS // 2, TQ_MLA, LANES), F32)],
        compiler_params=_params("parallel", "parallel", "arbitrary"),
        name="mla_attention",
    )(q, k, v)


def _band_bias(radius, tq):
    row = np.arange(tq)[:, None]
    col = np.arange(tq + 2 * T_HALO)[None, :]
    d = col - T_HALO - row
    return np.where(np.abs(d) <= radius, 0.0, NEG).astype(np.float32)


def _banded_kernel(*refs, tq, n_pairs, slot_of_pair, has_sink, emit_lse):
    refs = list(refs)
    sink_ref = refs.pop(0) if has_sink else None
    q_ref, kp_ref, kc_ref, kn_ref, vp_ref, vc_ref, vn_ref, bias_ref = refs[:8]
    o_ref = refs[8]
    lse_ref = refs[9] if emit_lse else None
    qi = pl.program_id(2)
    nb = pl.num_programs(2)
    col = lax.broadcasted_iota(jnp.int32, (1, tq + 2 * T_HALO), 1)
    no_prev = jnp.where(qi == 0, NEG, 0.0)
    no_next = jnp.where(qi == nb - 1, NEG, 0.0)
    bias = (bias_ref[...] + jnp.where(col < T_HALO, no_prev, 0.0)
            + jnp.where(col >= tq + T_HALO, no_next, 0.0))
    low = _low_half_lanes()
    bias2 = _tile_rows(bias, 2)
    for m in range(n_pairs):
        sl = slice(slot_of_pair(m) * LANES, (slot_of_pair(m) + 1) * LANES)
        kcat = jnp.concatenate([kp_ref[:, sl], kc_ref[:, sl], kn_ref[:, sl]], axis=0)
        vcat = jnp.concatenate([vp_ref[:, sl], vc_ref[:, sl], vn_ref[:, sl]], axis=0)
        qp = q_ref[:, m * LANES:(m + 1) * LANES]
        zero = jnp.zeros_like(qp)
        q2 = jnp.concatenate([jnp.where(low, qp, zero), jnp.where(low, zero, qp)], axis=0)
        s = lax.dot_general(q2, kcat, _NT, preferred_element_type=F32) + bias2
        mx = jnp.max(s, axis=-1, keepdims=True)
        if has_sink:
            first = lax.broadcasted_iota(jnp.int32, (2 * tq, 1), 0) < tq
            sk = jnp.where(first, sink_ref[2 * m], sink_ref[2 * m + 1])
            mx = jnp.maximum(mx, sk)
        p = jnp.exp(s - mx)
        den = jnp.sum(p, axis=-1, keepdims=True)
        if has_sink:
            den = den + jnp.exp(sk - mx)
        res = jnp.dot(p.astype(BF16), vcat, preferred_element_type=F32) / den
        o_ref[:, m * LANES:(m + 1) * LANES] = jnp.where(low, res[:tq], res[tq:]).astype(BF16)
        if emit_lse:
            lse = mx + jnp.log(den)
            lse_ref[:, m * LANES:(m + 1) * LANES] = jnp.where(low, lse[:tq], lse[tq:])


def _banded_attention(q, k, v, n_seq, seq, dil, radius, slot_of_pair, sink=None, emit_lse=False):
    n, qw = q.shape[0] * dil, q.shape[1] // dil
    kw = k.shape[1] // dil
    sub = seq // dil
    tq = min(TQ_BAND, sub)
    nb = sub // tq
    hq = tq // T_HALO
    nh = sub // T_HALO
    cur = lambda b, r, i: (b * nb + i, r)
    prev = lambda b, r, i: (b * nh + jnp.maximum(i * hq - 1, 0), r)
    nxt = lambda b, r, i: (b * nh + jnp.minimum((i + 1) * hq, nh - 1), r)
    halo = lambda f: pl.BlockSpec((T_HALO, kw), f)
    body = pl.BlockSpec((tq, kw), cur)
    in_specs = [pl.BlockSpec((tq, qw), cur),
                halo(prev), body, halo(nxt), halo(prev), body, halo(nxt),
                pl.BlockSpec((tq, tq + 2 * T_HALO), lambda b, r, i: (0, 0))]
    args = [q, k, k, k, v, v, v, jnp.asarray(_band_bias(radius, tq))]
    if sink is not None:
        in_specs = [pl.BlockSpec(memory_space=pltpu.SMEM)] + in_specs
        args = [sink] + args
    out_specs = [pl.BlockSpec((tq, qw), cur)]
    out_shape = [jax.ShapeDtypeStruct((n // dil, dil * qw), BF16)]
    if emit_lse:
        out_specs.append(pl.BlockSpec((tq, qw), cur))
        out_shape.append(jax.ShapeDtypeStruct((n // dil, dil * qw), F32))
    return pl.pallas_call(
        functools.partial(_banded_kernel, tq=tq, n_pairs=qw // LANES, slot_of_pair=slot_of_pair,
                          has_sink=sink is not None, emit_lse=emit_lse),
        grid=(n_seq, dil, nb),
        in_specs=in_specs, out_specs=out_specs, out_shape=out_shape,
        compiler_params=_params("parallel", "parallel", "parallel"),
        name="banded_attention",
    )(*args)


def _outproj_even_kernel(oa_ref, o1_ref, l1_ref, o4_ref, l4_ref, o16_ref, l16_ref, wa_ref, wb_ref,
                         x_ref, g_ref, b_ref, y_ref, po4_ref, pl4_ref, po16_ref, pl16_ref):
    def natural(src_ref, dst_ref, d):
        for r in range(d):
            for lt in range(DIL_W // LANES):
                c0 = r * DIL_W + lt * LANES
                dst_ref[lt, pl.ds(r, TM // d, stride=d), :] = src_ref[:, c0:c0 + LANES].astype(F32)
        return jnp.concatenate([dst_ref[lt] for lt in range(DIL_W // LANES)], axis=-1)

    l1, o1 = l1_ref[...], o1_ref[...].astype(F32)
    l2, o2 = natural(l4_ref, pl4_ref, 4), natural(o4_ref, po4_ref, 4)
    l3, o3 = natural(l16_ref, pl16_ref, 16), natural(o16_ref, po16_ref, 16)
    mx = jnp.maximum(jnp.maximum(l1, l2), l3)
    e1, e2, e3 = jnp.exp(l1 - mx), jnp.exp(l2 - mx), jnp.exp(l3 - mx)
    ob = (e1 * o1 + e2 * o2 + e3 * o3) / (e1 + e2 + e3)
    mix = (jnp.dot(oa_ref[...], wa_ref[...], preferred_element_type=F32)
           + jnp.dot(ob.astype(BF16), wb_ref[...], preferred_element_type=F32))
    y_ref[...] = _layer_norm(DN_ALPHA * x_ref[...] + mix, g_ref[...], b_ref[...])


def _outproj_even(oa, pats, wa, wb, x, g, b):
    n = x.shape[0]
    row = lambda i: (i, 0)
    pat_specs, pat_args = [], []
    for (_, d), (o, lse) in zip(DIL_PATTERNS, pats):
        pat_specs += [pl.BlockSpec((TM // d, d * DIL_W), row)] * 2
        pat_args += [o, lse]
    return pl.pallas_call(
        _outproj_even_kernel,
        grid=(n // TM,),
        in_specs=[pl.BlockSpec((TM, DIL_W), row)] + pat_specs
        + [_const_spec(wa.shape), _const_spec(wb.shape),
           pl.BlockSpec((TM, D_MODEL), row), _const_spec(g.shape), _const_spec(b.shape)],
        out_specs=pl.BlockSpec((TM, D_MODEL), row),
        out_shape=jax.ShapeDtypeStruct((n, D_MODEL), F32),
        scratch_shapes=[pltpu.VMEM((DIL_W // LANES, TM, LANES), F32)] * 4,
        compiler_params=_params("parallel"),
        name="outproj_even",
    )(oa, *pat_args, wa, wb, x, g, b)


def _outproj_odd_kernel(o_ref, w_ref, x_ref, g_ref, b_ref, y_ref):
    mix = jnp.dot(o_ref[...], w_ref[...], preferred_element_type=F32)
    y_ref[...] = _layer_norm(DN_ALPHA * x_ref[...] + mix, g_ref[...], b_ref[...])


def _outproj_odd(o, w, x, g, b):
    n = x.shape[0]
    row = lambda i: (i, 0)
    return pl.pallas_call(
        _outproj_odd_kernel,
        grid=(n // TM,),
        in_specs=[pl.BlockSpec((TM, o.shape[1]), row), _const_spec(w.shape),
                  pl.BlockSpec((TM, D_MODEL), row), _const_spec(g.shape), _const_spec(b.shape)],
        out_specs=pl.BlockSpec((TM, D_MODEL), row),
        out_shape=jax.ShapeDtypeStruct((n, D_MODEL), F32),
        compiler_params=_params("parallel"),
        name="outproj_odd",
    )(o, w, x, g, b)


def _inproj_odd_kernel(x_ref, wq_ref, wk_ref, wv_ref, tb_ref, q_ref, k_ref, v_ref):
    xb = x_ref[...].astype(BF16)
    dot = functools.partial(jnp.dot, preferred_element_type=F32)
    q12 = dot(xb, wq_ref[...])
    q_ref[...] = (q12[:, :1024] * _tile_lanes(tb_ref[0], 8)
                  + q12[:, 1024:] * _tile_lanes(tb_ref[1], 8)).astype(BF16)
    k12 = dot(xb, wk_ref[...])
    k_ref[...] = (k12[:, :512] * _tile_lanes(tb_ref[2], 4)
                  + k12[:, 512:] * _tile_lanes(tb_ref[3], 4)).astype(BF16)
    v_ref[...] = dot(xb, wv_ref[...]).astype(BF16)


def _inproj_odd(x, w, tb, seq):
    n = x.shape[0]
    pos_blocks = seq // TM
    row = lambda i: (i, 0)
    ws = [w["wq"], w["wk"], w["wv"]]
    out_w = (1024, 512, 512)
    return pl.pallas_call(
        _inproj_odd_kernel,
        grid=(n // TM,),
        in_specs=[pl.BlockSpec((TM, D_MODEL), row)] + [_const_spec(a.shape) for a in ws]
        + [pl.BlockSpec((4, TM, LANES), lambda i: (0, i % pos_blocks, 0))],
        out_specs=[pl.BlockSpec((TM, c), row) for c in out_w],
        out_shape=[jax.ShapeDtypeStruct((n, c), BF16) for c in out_w],
        compiler_params=_params("parallel"),
        name="inproj_odd",
    )(x, *ws, tb)


def _sorting_network(n):
    pairs = []

    def merge(lo, hi, r):
        step = r * 2
        if step < hi - lo:
            merge(lo, hi, step)
            merge(lo + r, hi, step)
            pairs.extend((i, i + r) for i in range(lo + r, hi - r, step))
        else:
            pairs.append((lo, lo + r))

    def sort(lo, hi):
        if hi - lo >= 1:
            mid = lo + (hi - lo) // 2
            sort(lo, mid)
            sort(mid + 1, hi)
            merge(lo, hi, 1)

    sort(0, n - 1)
    return pairs


def _top16_sorted(s):
    k = PEER_TOPK
    w = [s[g * SUBLANES:(g + 1) * SUBLANES, :] for g in range(s.shape[0] // SUBLANES)]
    w += [jnp.full_like(w[0], NEG)] * (k - len(w))

    def exchange(i, j):
        w[i], w[j] = jnp.maximum(w[i], w[j]), jnp.minimum(w[i], w[j])

    for i, j in _sorting_network(k):
        exchange(i, j)
    shift = SUBLANES // 2
    while shift >= 1:
        other = [pltpu.roll(t, shift, 0) for t in w]
        w = [jnp.maximum(w[p], other[k - 1 - p]) for p in range(k)]
        d = k // 2
        while d >= 1:
            for i in range(k):
                if i & d == 0:
                    exchange(i, i + d)
            d //= 2
        shift //= 2
    return w


def _split_bf16(t):
    hi = t.astype(BF16)
    return hi, (t - hi.astype(F32)).astype(BF16)


def _dot3(a_hi, a_lo, b_hi, b_lo, dims):
    d = functools.partial(lax.dot_general, dimension_numbers=dims, preferred_element_type=F32)
    return d(a_hi, b_hi) + (d(a_hi, b_lo) + d(a_lo, b_hi))


def _peer_select_kernel(x_ref, wqh_ref, wql_ref, kh_ref, kl_ref, e2_ref, thr_ref, e1_ref,
                        qh_ref, ql_ref, a_ref, b_ref):
    h = pl.program_id(1)

    @pl.when(h == 0)
    def _project():
        xh, xl = _split_bf16(x_ref[...])
        q = _dot3(wqh_ref[...], wql_ref[...], xh, xl, _NT)
        qh_ref[...], ql_ref[...] = _split_bf16(q)

    half = PEER_D_KEY // 2
    row0 = pl.multiple_of(h * PEER_D_KEY, PEER_D_KEY)
    mm = (((1,), (0,)), ((), ()))
    s1 = _dot3(kh_ref[0, 0], kl_ref[0, 0], qh_ref[pl.ds(row0, half), :], ql_ref[pl.ds(row0, half), :], mm)
    s2 = _dot3(kh_ref[0, 1], kl_ref[0, 1], qh_ref[pl.ds(row0 + half, half), :],
               ql_ref[pl.ds(row0 + half, half), :], mm)
    for lt in range(TM_SEL // LANES):
        ls = slice(lt * LANES, (lt + 1) * LANES)
        s1t, s2t = s1[:, ls], s2[:, ls]
        a_rep, b_rep = _top16_sorted(s1t), _top16_sorted(s2t)
        for p in range(PEER_TOPK):
            a_ref[p:p + 1, ls] = a_rep[p][0:1]
            b_ref[p:p + 1, ls] = b_rep[p][0:1]
        a16, b16 = a_ref[:, ls], b_ref[:, ls]
        a0, b0 = a16[0:1], b16[0:1]
        cand = [a0 + b16]
        cand += [a16[i:i + 1] + b16[0:8] for i in range(1, 8)]
        cand += [a16[8:16] + b0]
        cand = jnp.concatenate(cand, axis=0)
        v16 = _top16_sorted(cand)[PEER_TOPK - 1][0:1]
        z = jnp.sum(jnp.where(cand >= v16, jnp.exp(cand - (a0 + b0)), 0.0), axis=0, keepdims=True)
        half_inv_z = 0.5 / z
        e2_sorted = jnp.exp(b16 - b0) * half_inv_z
        thr = jnp.full(s1t.shape, NO_KEY_QUALIFIES, F32)
        for i in range(PEER_TOPK):
            t_i = jnp.min(jnp.where(a16[i:i + 1] + b16 >= v16, e2_sorted, NO_KEY_QUALIFIES), axis=0, keepdims=True)
            thr = jnp.where(s1t == _tile_rows(a_rep[i], PEER_N_KEYS // SUBLANES), t_i, thr)
        e2_ref[:, ls] = jnp.exp(s2t - b0) * half_inv_z
        thr_ref[:, ls] = thr
        e1_ref[:, ls] = jnp.exp(s1t - a0)


def _peer_select(x, wq_hi, wq_lo, keys_hi, keys_lo):
    n = x.shape[0]
    spec = pl.BlockSpec((PEER_N_KEYS, TM_SEL), lambda i, h: (h, i))
    shape = lambda dt: jax.ShapeDtypeStruct((PEER_HEADS * PEER_N_KEYS, n), dt)
    kspec = pl.BlockSpec((1, 2, PEER_N_KEYS, PEER_D_KEY // 2), lambda i, h: (h, 0, 0, 0))
    return pl.pallas_call(
        _peer_select_kernel,
        grid=(n // TM_SEL, PEER_HEADS),
        in_specs=[pl.BlockSpec((TM_SEL, D_MODEL), lambda i, h: (i, 0)),
                  _const_spec(wq_hi.shape), _const_spec(wq_lo.shape), kspec, kspec],
        out_specs=[spec] * 3,
        out_shape=[shape(F32)] * 3,
        scratch_shapes=[pltpu.VMEM((PEER_HEADS * PEER_D_KEY, TM_SEL), BF16),
                        pltpu.VMEM((PEER_HEADS * PEER_D_KEY, TM_SEL), BF16),
                        pltpu.VMEM((PEER_TOPK, TM_SEL), F32),
                        pltpu.VMEM((PEER_TOPK, TM_SEL), F32)],
        compiler_params=_params("parallel", "arbitrary"),
        name="peer_select",
    )(x, wq_hi, wq_lo, keys_hi, keys_lo)


def _peer_expert_kernel(x_ref, e2_ref, thr_ref, e1_ref, u_ref, vt_ref, g_ref, b_ref, y_ref,
                        xb_ref, a_ref, h_ref, acc_ref):
    eb = pl.program_id(1)
    n_i1 = EB_PEER // PEER_N_KEYS
    halves = 2
    rows_half = EB_PEER // halves
    n_r = 2
    slab = PEER_N_KEYS // 2

    @pl.when(eb == 0)
    def _init():
        xb_ref[...] = x_ref[...].astype(BF16)
        acc_ref[...] = jnp.zeros(acc_ref.shape, F32)

    for hh in range(halves):
        rs = slice(hh * rows_half, (hh + 1) * rows_half)
        a_ref[rs, :] = lax.dot_general(u_ref[rs, :], xb_ref[...], _NT, preferred_element_type=F32)
    i1_base = pl.multiple_of(eb * n_i1, SUBLANES)
    for hh in range(halves):
        for lt in range(TM_PEER // LANES):
            ls = slice(lt * LANES, (lt + 1) * LANES)
            for rp in range(n_i1 // halves // n_r):
                r0 = hh * (n_i1 // halves) + n_r * rp
                for sb in range(PEER_N_KEYS // slab):
                    gate = [jnp.zeros((slab, LANES), F32)] * n_r
                    for h in range(PEER_HEADS):
                        t8 = thr_ref[pl.ds(h * PEER_N_KEYS + i1_base, n_i1), ls]
                        e8 = e1_ref[pl.ds(h * PEER_N_KEYS + i1_base, n_i1), ls]
                        e2t = e2_ref[h * PEER_N_KEYS + sb * slab:h * PEER_N_KEYS + (sb + 1) * slab, ls]
                        for t in range(n_r):
                            r = r0 + t
                            gate[t] = gate[t] + jnp.where(e2t >= t8[r:r + 1], e2t, 0.0) * e8[r:r + 1]
                    for t in range(n_r):
                        rows = slice((r0 + t) * PEER_N_KEYS + sb * slab, (r0 + t) * PEER_N_KEYS + (sb + 1) * slab)
                        a = a_ref[rows, ls]
                        h_ref[rows, ls] = (a * (1.0 + lax.erf(a * INV_SQRT2)) * gate[t]).astype(BF16)
        rs = slice(hh * rows_half, (hh + 1) * rows_half)
        acc_ref[...] += jnp.dot(vt_ref[:, rs], h_ref[rs, :], preferred_element_type=F32)

    @pl.when(eb == pl.num_programs(1) - 1)
    def _fin():
        y = DN_ALPHA * x_ref[...] + acc_ref[...].T
        y_ref[...] = _layer_norm(y, g_ref[...], b_ref[...])


def _peer_experts(x, stats, u_bf, vt_bf, layer, g, b):
    n = x.shape[0]
    stat_spec = pl.BlockSpec((PEER_HEADS * PEER_N_KEYS, TM_PEER), lambda i, e: (0, i))
    return pl.pallas_call(
        _peer_expert_kernel,
        grid=(n // TM_PEER, PEER_N_EXPERTS // EB_PEER),
        in_specs=[pl.BlockSpec((TM_PEER, D_MODEL), lambda i, e: (i, 0))] + [stat_spec] * 3
        + [pl.BlockSpec((None, EB_PEER, D_MODEL), lambda i, e: (layer, e, 0)),
           pl.BlockSpec((None, D_MODEL, EB_PEER), lambda i, e: (layer, 0, e)),
           _const_spec(g.shape), _const_spec(b.shape)],
        out_specs=pl.BlockSpec((TM_PEER, D_MODEL), lambda i, e: (i, 0)),
        out_shape=jax.ShapeDtypeStruct((n, D_MODEL), F32),
        scratch_shapes=[pltpu.VMEM((TM_PEER, D_MODEL), BF16),
                        pltpu.VMEM((EB_PEER, TM_PEER), F32),
                        pltpu.VMEM((EB_PEER, TM_PEER), BF16),
                        pltpu.VMEM((D_MODEL, TM_PEER), F32)],
        compiler_params=_params("parallel", "arbitrary"),
        name="peer_experts",
    )(x, *stats, u_bf, vt_bf, g, b)


def _ple_kernel(x_ref, p_ref, wp_ref, wg_ref, y_ref):
    x = x_ref[...]
    proj = jnp.dot(p_ref[...].astype(BF16), wp_ref[...], preferred_element_type=F32)
    gate = jax.nn.sigmoid(jnp.dot(x.astype(BF16), wg_ref[...], preferred_element_type=F32))
    y_ref[...] = x + proj * gate


def _ple(x, p, layer, wp, wg):
    n = x.shape[0]
    row = lambda i: (i, 0)
    return pl.pallas_call(
        _ple_kernel,
        grid=(n // TM,),
        in_specs=[pl.BlockSpec((TM, D_MODEL), row), pl.BlockSpec((None, TM, PLE_DIM), lambda i: (layer, i, 0)),
                  _const_spec(wp.shape), _const_spec(wg.shape)],
        out_specs=pl.BlockSpec((TM, D_MODEL), row),
        out_shape=jax.ShapeDtypeStruct((n, D_MODEL), F32),
        compiler_params=_params("parallel"),
        name="ple_gate",
    )(x, p, wp, wg)


def _rope_partner_cols(w, head_dim, half):
    n_heads = w.shape[1] // head_dim
    w3 = w.reshape(w.shape[0], n_heads, head_dim)
    sw = jnp.concatenate([w3[..., half:2 * half], w3[..., :half],
                          jnp.zeros_like(w3[..., 2 * half:])], axis=-1)
    return sw.reshape(w.shape)


def _twice_per_head(w, head_dim):
    n_heads = w.shape[1] // head_dim
    w3 = w.reshape(w.shape[0], n_heads, head_dim)
    return jnp.concatenate([w3, w3], axis=-1).reshape(w.shape[0], 2 * w.shape[1])


def _rope_tables(seq, half, theta, lanes_before, lanes_after, pass_through, scale):
    inv = theta ** (-jnp.arange(half, dtype=F32) / half)
    ang = jnp.arange(seq).astype(F32)[:, None] * inv[None, :]
    cos, sin = jnp.cos(ang), jnp.sin(ang)
    ones = lambda w, v: jnp.full((seq, w), v, F32)
    ct = jnp.concatenate([ones(lanes_before, pass_through), cos, cos, ones(lanes_after, pass_through)], axis=-1)
    st = jnp.concatenate([ones(lanes_before, 0.0), -sin, sin, ones(lanes_after, 0.0)], axis=-1)
    return ct * scale, st * scale


def _prep_even(a_w_in, a_q_norm, a_kv_norm, a_w_uq, a_w_ukv, a_w_out):
    o1 = MLA_Q_LORA
    o2 = o1 + MLA_KV_LORA
    o3 = o2 + MLA_ROPE
    nb = DIL_HEADS * DIL_HD
    bf = lambda t: t.astype(BF16)
    w = {}
    w["wcq"] = bf(a_w_in[:, :o1])
    w["wckv"] = bf(a_w_in[:, o1:o2])
    kr = a_w_in[:, o2:o3]
    krs = jnp.concatenate([kr[:, MLA_ROPE // 2:], kr[:, :MLA_ROPE // 2]], axis=-1)
    pad_slot = lambda t: jnp.pad(t, ((0, 0), (MLA_NOPE, MLA_SLOT - MLA_NOPE - MLA_ROPE)))
    w["wkr"] = bf(jnp.concatenate([pad_slot(kr), pad_slot(krs)], axis=-1))
    qb = a_w_in[:, o3:o3 + nb]
    kb = a_w_in[:, o3 + nb:o3 + 2 * nb]
    rd2 = DIL_HD // ROPE_FRACTION // 2
    w["wqb"] = bf(jnp.concatenate([qb, _rope_partner_cols(qb, DIL_HD, rd2)], axis=-1))
    w["wkb"] = bf(jnp.concatenate([kb, _rope_partner_cols(kb, DIL_HD, rd2)], axis=-1))
    w["wvb"] = bf(a_w_in[:, o3 + 2 * nb:])
    w["qn"] = a_q_norm.reshape(1, -1)
    w["kvn"] = a_kv_norm.reshape(1, -1)
    uq = a_w_uq.reshape(MLA_Q_LORA, MLA_HEADS, MLA_NOPE + MLA_ROPE)
    zpad = jnp.zeros((MLA_Q_LORA, MLA_HEADS, MLA_SLOT - MLA_NOPE - MLA_ROPE), F32)
    uq1 = jnp.concatenate([uq, zpad], axis=-1).reshape(MLA_Q_LORA, MLA_HEADS * MLA_SLOT)
    r = uq[..., MLA_NOPE:]
    uq2 = jnp.concatenate([jnp.zeros_like(uq[..., :MLA_NOPE]), r[..., MLA_ROPE // 2:], r[..., :MLA_ROPE // 2],
                           zpad], axis=-1).reshape(MLA_Q_LORA, MLA_HEADS * MLA_SLOT)
    w["wuq"] = bf(jnp.concatenate([uq1, uq2], axis=-1))
    ukv = a_w_ukv.reshape(MLA_KV_LORA, MLA_HEADS, MLA_NOPE + MLA_V)
    kpad = jnp.zeros((MLA_KV_LORA, MLA_HEADS, MLA_SLOT - MLA_NOPE), F32)
    w["wk"] = bf(jnp.concatenate([ukv[..., :MLA_NOPE], kpad], axis=-1).reshape(MLA_KV_LORA, -1))
    w["wv"] = bf(ukv[..., MLA_NOPE:].reshape(MLA_KV_LORA, MLA_HEADS * MLA_V))
    w["wout_a"] = bf(a_w_out[:MLA_HEADS * MLA_V])
    w["wout_b"] = bf(a_w_out[MLA_HEADS * MLA_V:])
    return w


def _prep_odd(c_w_in, c_w_out):
    e1 = C_HEADS * C_HD
    e2 = e1 + C_KV_HEADS * C_HD
    rd2 = C_HD // ROPE_FRACTION // 2
    q, k = c_w_in[:, :e1], c_w_in[:, e1:e2]
    k2 = lambda t: _twice_per_head(t, C_HD)
    return {
        "wq": jnp.concatenate([q, _rope_partner_cols(q, C_HD, rd2)], axis=-1).astype(BF16),
        "wk": jnp.concatenate([k2(k), k2(_rope_partner_cols(k, C_HD, rd2))], axis=-1).astype(BF16),
        "wv": k2(c_w_in[:, e2:]).astype(BF16),
        "wout": c_w_out.astype(BF16),
    }


def _split_kernel(w_ref, hi_ref, lo_ref):
    hi_ref[...], lo_ref[...] = _split_bf16(w_ref[...])


def _split_hi_lo(w):
    flat = w.reshape(-1, LANES)
    rows = flat.shape[0]
    blk = min(rows, 1024)
    spec = pl.BlockSpec((blk, LANES), lambda i: (i, 0))
    hi, lo = pl.pallas_call(
        _split_kernel,
        grid=(rows // blk,),
        in_specs=[spec], out_specs=[spec, spec],
        out_shape=[jax.ShapeDtypeStruct(flat.shape, BF16)] * 2,
        compiler_params=_params("parallel"),
        name="split_hi_lo",
    )(flat)
    return hi.reshape(w.shape), lo.reshape(w.shape)


def _prep_peer(peer_w_q, peer_sub_keys, peer_u, peer_v):
    wq_hi, wq_lo = _split_hi_lo(jnp.swapaxes(peer_w_q, 1, 2))
    keys_hi, keys_lo = _split_hi_lo(peer_sub_keys)
    return {
        "wq_hi": wq_hi, "wq_lo": wq_lo, "keys_hi": keys_hi, "keys_lo": keys_lo,
        "u": peer_u.astype(BF16), "vt": jnp.swapaxes(peer_v, 1, 2).astype(BF16),
    }


def _tables_mla(seq):
    scale = (MLA_NOPE + MLA_ROPE) ** -0.5
    pad = MLA_SLOT - MLA_NOPE - MLA_ROPE
    cq, sq = _rope_tables(seq, MLA_ROPE // 2, MLA_THETA, MLA_NOPE, pad, 1.0, scale)
    ck, sk = _rope_tables(seq, MLA_ROPE // 2, MLA_THETA, MLA_NOPE, pad, 0.0, 1.0)
    return jnp.stack([cq, sq, ck, sk])


def _tables_partial(seq, head_dim, q_scale):
    half = head_dim // ROPE_FRACTION // 2
    cq, sq = _rope_tables(seq, half, ROPE_THETA, 0, head_dim - 2 * half, 1.0, q_scale)
    ck, sk = _rope_tables(seq, half, ROPE_THETA, 0, head_dim - 2 * half, 1.0, 1.0)
    two = lambda t: jnp.concatenate([t, t], axis=-1)
    return jnp.stack([two(cq), two(sq), two(ck), two(sk)])


def _peer_layer(x, pw, i, g, b):
    stats = _peer_select(x, pw["wq_hi"][i], pw["wq_lo"][i], pw["keys_hi"][i], pw["keys_lo"][i])
    return _peer_experts(x, stats, pw["u"], pw["vt"], i, g, b)


def _mixer_even(x, w, n_seq, seq, g, b):
    outs = _inproj_even(x, w, _tables_mla(seq), _tables_partial(seq, DIL_HD, DIL_HD ** -0.5), seq)
    qa, ka, va = outs[:3]
    oa = _mla_attention(qa, ka, va, n_seq, seq)
    pats = []
    for p, (window, dil) in enumerate(DIL_PATTERNS):
        qd, kd, vd = outs[3 + 3 * p:6 + 3 * p]
        pats.append(_banded_attention(qd, kd, vd, n_seq, seq, dil, (window // 2) // dil, lambda m: m, emit_lse=True))
    return _outproj_even(oa, pats, w["wout_a"], w["wout_b"], x, g, b)


def _mixer_odd(x, w, sink, n_seq, seq, g, b):
    q, k, v = _inproj_odd(x, w, _tables_partial(seq, C_HD, C_HD ** -0.5), seq)
    pairs_per_group = C_HEADS // C_KV_HEADS // 2
    (o,) = _banded_attention(q, k, v, n_seq, seq, 1, C_RADIUS, lambda m: m // pairs_per_group, sink=sink)
    return _outproj_odd(o, w["wout"], x, g, b)


def _trunk(x3, p4, prm):
    n_seq, seq, _ = x3.shape
    x = x3.reshape(n_seq * seq, D_MODEL)
    p = p4.reshape(DEPTH, n_seq * seq, PLE_DIM)
    row = lambda t: t.reshape(1, -1)
    for i in range(DEPTH):
        g, b = row(prm["ln_mix_g"][i]), row(prm["ln_mix_b"][i])
        if i % 2 == 0:
            x = _mixer_even(x, prm["even"], n_seq, seq, g, b)
        else:
            x = _mixer_odd(x, prm["odd"], prm["sink"], n_seq, seq, g, b)
        x = _peer_layer(x, prm["peer"], i, row(prm["ln_ffn_g"][i]), row(prm["ln_ffn_b"][i]))
        x = _ple(x, p, i, prm["ple_proj"][i], prm["ple_gate"][i])
    return x.reshape(n_seq, seq, D_MODEL)


def kernel(x_prompt, x_sample, p_prompt, p_sample, a_w_in, a_q_norm, a_kv_norm, a_w_uq, a_w_ukv, a_w_out,
           c_w_in, c_sink, c_w_out, ln_mix_g, ln_mix_b, ln_ffn_g, ln_ffn_b, peer_w_q, peer_sub_keys,
           peer_u, peer_v, ple_proj, ple_gate):
    prm = {
        "even": _prep_even(a_w_in[0], a_q_norm[0], a_kv_norm[0], a_w_uq[0], a_w_ukv[0], a_w_out[0]),
        "odd": _prep_odd(c_w_in[0], c_w_out[0]),
        "sink": c_sink[0],
        "ln_mix_g": ln_mix_g, "ln_mix_b": ln_mix_b, "ln_ffn_g": ln_ffn_g, "ln_ffn_b": ln_ffn_b,
        "peer": _prep_peer(peer_w_q, peer_sub_keys, peer_u, peer_v),
        "ple_proj": ple_proj.astype(BF16),
        "ple_gate": ple_gate.astype(BF16),
    }
    y_prompt = _trunk(x_prompt, p_prompt, prm)
    y_sample = _trunk(x_sample, p_sample, prm)
    return (y_prompt, y_sample)
```

```python
import functools

import jax
import jax.numpy as jnp
import numpy as np
from jax import lax
from jax.experimental import pallas as pl
from jax.experimental.pallas import tpu as pltpu

F32 = jnp.float32
BF16 = jnp.bfloat16

D_MODEL = 1024
DEPTH = 2
PLE_DIM = 256
MLA_HEADS = 8
MLA_Q_LORA = 384
MLA_KV_LORA = 256
MLA_NOPE = 64
MLA_ROPE = 32
MLA_V = 64
MLA_THETA = 10000.0
MLA_SLOT = 128
DIL_HEADS = 8
DIL_HD = 64
DIL_PATTERNS = ((128, 1), (512, 4), (2048, 16))
DIL_W = DIL_HEADS * DIL_HD
C_HEADS = 16
C_KV_HEADS = 4
C_HD = 64
C_RADIUS = 128
ROPE_THETA = 500000.0
ROPE_FRACTION = 4
PEER_HEADS = 8
PEER_N_KEYS = 128
PEER_N_EXPERTS = PEER_N_KEYS * PEER_N_KEYS
PEER_D_KEY = 128
PEER_TOPK = 16
DN_ALPHA = (2 * DEPTH) ** 0.25
LN_EPS = 1e-5
RMS_EPS = 1e-6
NEG = -1e30
INV_SQRT2 = 0.7071067811865476
NO_KEY_QUALIFIES = 1.0

LANES = 128
SUBLANES = 8
VMEM_LIMIT_BYTES = 56 * 1024 * 1024

TM = 512
TQ_MLA = 512
TK_MLA = 2048
TQ_BAND = 256
T_HALO = 128
TM_SEL = 1024
TM_PEER = 512
EB_PEER = 2048

_NT = (((1,), (1,)), ((), ()))


def _params(*sem):
    return pltpu.CompilerParams(dimension_semantics=sem, vmem_limit_bytes=VMEM_LIMIT_BYTES)


def _layer_norm(y, g, b):
    mu = jnp.mean(y, axis=-1, keepdims=True)
    d = y - mu
    var = jnp.mean(d * d, axis=-1, keepdims=True)
    return d * lax.rsqrt(var + LN_EPS) * g + b


def _rms_norm(y, g):
    return y * lax.rsqrt(jnp.mean(y * y, axis=-1, keepdims=True) + RMS_EPS) * g


def _tile_lanes(t, reps):
    return jnp.concatenate([t] * reps, axis=-1)


def _tile_rows(t, reps):
    return jnp.concatenate([t] * reps, axis=0)


def _low_half_lanes():
    return lax.broadcasted_iota(jnp.int32, (1, LANES), 1) < LANES // 2


def _const_spec(shape):
    nd = len(shape)
    return pl.BlockSpec(shape, lambda *_: (0,) * nd)


def _inproj_even_kernel(x_ref, wcq_ref, wckv_ref, wkr_ref, wqb_ref, wkb_ref, wvb_ref,
                        qn_ref, kvn_ref, wuq_ref, wk_ref, wv_ref, ta_ref, tb_ref,
                        qa_ref, ka_ref, va_ref, qb_ref, kb_ref, vb_ref,
                        q4_ref, k4_ref, v4_ref, q16_ref, k16_ref, v16_ref, perm_ref):
    xb = x_ref[...].astype(BF16)
    dot = functools.partial(jnp.dot, preferred_element_type=F32)
    cq = _rms_norm(dot(xb, wcq_ref[...]), qn_ref[...]).astype(BF16)
    q12 = dot(cq, wuq_ref[...])
    cq_t = _tile_lanes(ta_ref[0], MLA_HEADS)
    sq_t = _tile_lanes(ta_ref[1], MLA_HEADS)
    qa_ref[...] = (q12[:, :1024] * cq_t + q12[:, 1024:] * sq_t).astype(BF16)
    ckv = _rms_norm(dot(xb, wckv_ref[...]), kvn_ref[...]).astype(BF16)
    kr12 = dot(xb, wkr_ref[...])
    kr = kr12[:, :LANES] * ta_ref[2] + kr12[:, LANES:] * ta_ref[3]
    ka_ref[...] = (dot(ckv, wk_ref[...]) + _tile_lanes(kr, MLA_HEADS)).astype(BF16)
    va_ref[...] = dot(ckv, wv_ref[...]).astype(BF16)
    def emit(val, nat_ref, views):
        nat_ref[...] = val.astype(BF16)
        for lt in range(DIL_W // LANES):
            perm_ref[lt] = val[:, lt * LANES:(lt + 1) * LANES]
        for d, ref in views:
            for r in range(d):
                for lt in range(DIL_W // LANES):
                    c0 = r * DIL_W + lt * LANES
                    ref[:, c0:c0 + LANES] = perm_ref[lt, pl.ds(r, TM // d, stride=d), :].astype(BF16)

    q12b = dot(xb, wqb_ref[...])
    emit(q12b[:, :512] * _tile_lanes(tb_ref[0], 4) + q12b[:, 512:] * _tile_lanes(tb_ref[1], 4),
         qb_ref, ((4, q4_ref), (16, q16_ref)))
    k12b = dot(xb, wkb_ref[...])
    emit(k12b[:, :512] * _tile_lanes(tb_ref[2], 4) + k12b[:, 512:] * _tile_lanes(tb_ref[3], 4),
         kb_ref, ((4, k4_ref), (16, k16_ref)))
    emit(dot(xb, wvb_ref[...]), vb_ref, ((4, v4_ref), (16, v16_ref)))


def _inproj_even(x, w, ta, tb, seq):
    n = x.shape[0]
    pos_blocks = seq // TM
    row = lambda i: (i, 0)
    tab = lambda i: (0, i % pos_blocks, 0)
    wnames = ("wcq", "wckv", "wkr", "wqb", "wkb", "wvb", "qn", "kvn", "wuq", "wk", "wv")
    ws = [w[k] for k in wnames]
    out_w = (1024, 1024, 512, 512, 512, 512)
    views = [4, 4, 4, 16, 16, 16]
    return pl.pallas_call(
        _inproj_even_kernel,
        grid=(n // TM,),
        in_specs=[pl.BlockSpec((TM, D_MODEL), row)] + [_const_spec(a.shape) for a in ws]
        + [pl.BlockSpec((4, TM, LANES), tab), pl.BlockSpec((4, TM, LANES), tab)],
        out_specs=[pl.BlockSpec((TM, c), row) for c in out_w]
        + [pl.BlockSpec((TM // d, d * DIL_W), row) for d in views],
        out_shape=[jax.ShapeDtypeStruct((n, c), BF16) for c in out_w]
        + [jax.ShapeDtypeStruct((n // d, d * DIL_W), BF16) for d in views],
        scratch_shapes=[pltpu.VMEM((DIL_W // LANES, TM, LANES), F32)],
        compiler_params=_params("parallel"),
        name="inproj_even",
    )(x, *ws, ta, tb)


def _mla_kernel(q_ref, k_ref, v_ref, o_ref, m_ref, l_ref, acc_ref):
    ki = pl.program_id(2)
    low = _low_half_lanes()
    reps = TK_MLA // LANES

    @pl.when(ki == 0)
    def _init():
        m_ref[...] = jnp.full(m_ref.shape, NEG, F32)
        l_ref[...] = jnp.zeros(l_ref.shape, F32)
        acc_ref[...] = jnp.zeros(acc_ref.shape, F32)

    for j in range(MLA_HEADS // 2):
        vslot = v_ref[:, j * LANES:(j + 1) * LANES]
        prod, alphas = [], []
        for half in range(2):
            h = 2 * j + half
            qh = q_ref[:, h * MLA_SLOT:(h + 1) * MLA_SLOT]
            kh = k_ref[:, h * MLA_SLOT:(h + 1) * MLA_SLOT]
            s = lax.dot_general(qh, kh, _NT, preferred_element_type=F32)
            m_prev = m_ref[h]
            m_new = jnp.maximum(m_prev, jnp.max(s, axis=-1, keepdims=True))
            alpha = jnp.exp(m_prev - m_new)
            p = jnp.exp(s - _tile_lanes(m_new, reps))
            l_ref[h] = alpha * l_ref[h] + jnp.sum(p, axis=-1, keepdims=True)
            m_ref[h] = m_new
            prod.append(jnp.dot(p.astype(BF16), vslot, preferred_element_type=F32))
            alphas.append(alpha)
        acc_ref[j] = (jnp.where(low, alphas[0], alphas[1]) * acc_ref[j]
                      + jnp.where(low, prod[0], prod[1]))

    @pl.when(ki == pl.num_programs(2) - 1)
    def _fin():
        outs = [acc_ref[j] / jnp.where(low, l_ref[2 * j], l_ref[2 * j + 1]) for j in range(MLA_HEADS // 2)]
        o_ref[...] = jnp.concatenate(outs, axis=-1).astype(BF16)


def _mla_attention(q, k, v, n_seq, seq):
    n = q.shape[0]
    nq, nk = seq // TQ_MLA, seq // TK_MLA
    return pl.pallas_call(
        _mla_kernel,
        grid=(n_seq, nq, nk),
        in_specs=[pl.BlockSpec((TQ_MLA, 1024), lambda b, i, j: (b * nq + i, 0)),
                  pl.BlockSpec((TK_MLA, 1024), lambda b, i, j: (b * nk + j, 0)),
                  pl.BlockSpec((TK_MLA, 512), lambda b, i, j: (b * nk + j, 0))],
        out_specs=pl.BlockSpec((TQ_MLA, 512), lambda b, i, j: (b * nq + i, 0)),
        out_shape=jax.ShapeDtypeStruct((n, 512), BF16),
        scratch_shapes=[pltpu.VMEM((MLA_HEADS, TQ_MLA, LANES), F32),
                        pltpu.VMEM((MLA_HEADS, TQ_MLA, LANES), F32),
                        pltpu.VMEM((MLA_HEADS // 2, TQ_MLA, LANES), F32)],
        compiler_params=_params("parallel", "parallel", "arbitrary"),
        name="mla_attention",
    )(q, k, v)


def _band_bias(radius, tq):
    row = np.arange(tq)[:, None]
    col = np.arange(tq + 2 * T_HALO)[None, :]
    d = col - T_HALO - row
    return np.where(np.abs(d) <= radius, 0.0, NEG).astype(np.float32)


def _banded_kernel(*refs, tq, n_pairs, slot_of_pair, has_sink, emit_lse):
    refs = list(refs)
    sink_ref = refs.pop(0) if has_sink else None
    q_ref, kp_ref, kc_ref, kn_ref, vp_ref, vc_ref, vn_ref, bias_ref = refs[:8]
    o_ref = refs[8]
    lse_ref = refs[9] if emit_lse else None
    qi = pl.program_id(2)
    nb = pl.num_programs(2)
    col = lax.broadcasted_iota(jnp.int32, (1, tq + 2 * T_HALO), 1)
    no_prev = jnp.where(qi == 0, NEG, 0.0)
    no_next = jnp.where(qi == nb - 1, NEG, 0.0)
    bias = (bias_ref[...] + jnp.where(col < T_HALO, no_prev, 0.0)
            + jnp.where(col >= tq + T_HALO, no_next, 0.0))
    low = _low_half_lanes()
    bias2 = _tile_rows(bias, 2)
    for m in range(n_pairs):
        sl = slice(slot_of_pair(m) * LANES, (slot_of_pair(m) + 1) * LANES)
        kcat = jnp.concatenate([kp_ref[:, sl], kc_ref[:, sl], kn_ref[:, sl]], axis=0)
        vcat = jnp.concatenate([vp_ref[:, sl], vc_ref[:, sl], vn_ref[:, sl]], axis=0)
        qp = q_ref[:, m * LANES:(m + 1) * LANES]
        zero = jnp.zeros_like(qp)
        q2 = jnp.concatenate([jnp.where(low, qp, zero), jnp.where(low, zero, qp)], axis=0)
        s = lax.dot_general(q2, kcat, _NT, preferred_element_type=F32) + bias2
        mx = jnp.max(s, axis=-1, keepdims=True)
        if has_sink:
            first = lax.broadcasted_iota(jnp.int32, (2 * tq, 1), 0) < tq
            sk = jnp.where(first, sink_ref[2 * m], sink_ref[2 * m + 1])
            mx = jnp.maximum(mx, sk)
        p = jnp.exp(s - mx)
        den = jnp.sum(p, axis=-1, keepdims=True)
        if has_sink:
            den = den + jnp.exp(sk - mx)
        res = jnp.dot(p.astype(BF16), vcat, preferred_element_type=F32) / den
        o_ref[:, m * LANES:(m + 1) * LANES] = jnp.where(low, res[:tq], res[tq:]).astype(BF16)
        if emit_lse:
            lse = mx + jnp.log(den)
            lse_ref[:, m * LANES:(m + 1) * LANES] = jnp.where(low, lse[:tq], lse[tq:])


def _banded_attention(q, k, v, n_seq, seq, dil, radius, slot_of_pair, sink=None, emit_lse=False):
    n, qw = q.shape[0] * dil, q.shape[1] // dil
    kw = k.shape[1] // dil
    sub = seq // dil
    tq = min(TQ_BAND, sub)
    nb = sub // tq
    hq = tq // T_HALO
    nh = sub // T_HALO
    cur = lambda b, r, i: (b * nb + i, r)
    prev = lambda b, r, i: (b * nh + jnp.maximum(i * hq - 1, 0), r)
    nxt = lambda b, r, i: (b * nh + jnp.minimum((i + 1) * hq, nh - 1), r)
    halo = lambda f: pl.BlockSpec((T_HALO, kw), f)
    body = pl.BlockSpec((tq, kw), cur)
    in_specs = [pl.BlockSpec((tq, qw), cur),
                halo(prev), body, halo(nxt), halo(prev), body, halo(nxt),
                pl.BlockSpec((tq, tq + 2 * T_HALO), lambda b, r, i: (0, 0))]
    args = [q, k, k, k, v, v, v, jnp.asarray(_band_bias(radius, tq))]
    if sink is not None:
        in_specs = [pl.BlockSpec(memory_space=pltpu.SMEM)] + in_specs
        args = [sink] + args
    out_specs = [pl.BlockSpec((tq, qw), cur)]
    out_shape = [jax.ShapeDtypeStruct((n // dil, dil * qw), BF16)]
    if emit_lse:
        out_specs.append(pl.BlockSpec((tq, qw), cur))
        out_shape.append(jax.ShapeDtypeStruct((n // dil, dil * qw), F32))
    return pl.pallas_call(
        functools.partial(_banded_kernel, tq=tq, n_pairs=qw // LANES, slot_of_pair=slot_of_pair,
                          has_sink=sink is not None, emit_lse=emit_lse),
        grid=(n_seq, dil, nb),
        in_specs=in_specs, out_specs=out_specs, out_shape=out_shape,
        compiler_params=_params("parallel", "parallel", "parallel"),
        name="banded_attention",
    )(*args)


def _outproj_even_kernel(oa_ref, o1_ref, l1_ref, o4_ref, l4_ref, o16_ref, l16_ref, wa_ref, wb_ref,
                         x_ref, g_ref, b_ref, y_ref, po4_ref, pl4_ref, po16_ref, pl16_ref):
    def natural(src_ref, dst_ref, d):
        for r in range(d):
            for lt in range(DIL_W // LANES):
                c0 = r * DIL_W + lt * LANES
                dst_ref[lt, pl.ds(r, TM // d, stride=d), :] = src_ref[:, c0:c0 + LANES].astype(F32)
        return jnp.concatenate([dst_ref[lt] for lt in range(DIL_W // LANES)], axis=-1)

    l1, o1 = l1_ref[...], o1_ref[...].astype(F32)
    l2, o2 = natural(l4_ref, pl4_ref, 4), natural(o4_ref, po4_ref, 4)
    l3, o3 = natural(l16_ref, pl16_ref, 16), natural(o16_ref, po16_ref, 16)
    mx = jnp.maximum(jnp.maximum(l1, l2), l3)
    e1, e2, e3 = jnp.exp(l1 - mx), jnp.exp(l2 - mx), jnp.exp(l3 - mx)
    ob = (e1 * o1 + e2 * o2 + e3 * o3) / (e1 + e2 + e3)
    mix = (jnp.dot(oa_ref[...], wa_ref[...], preferred_element_type=F32)
           + jnp.dot(ob.astype(BF16), wb_ref[...], preferred_element_type=F32))
    y_ref[...] = _layer_norm(DN_ALPHA * x_ref[...] + mix, g_ref[...], b_ref[...])


def _outproj_even(oa, pats, wa, wb, x, g, b):
    n = x.shape[0]
    row = lambda i: (i, 0)
    pat_specs, pat_args = [], []
    for (_, d), (o, lse) in zip(DIL_PATTERNS, pats):
        pat_specs += [pl.BlockSpec((TM // d, d * DIL_W), row)] * 2
        pat_args += [o, lse]
    return pl.pallas_call(
        _outproj_even_kernel,
        grid=(n // TM,),
        in_specs=[pl.BlockSpec((TM, DIL_W), row)] + pat_specs
        + [_const_spec(wa.shape), _const_spec(wb.shape),
           pl.BlockSpec((TM, D_MODEL), row), _const_spec(g.shape), _const_spec(b.shape)],
        out_specs=pl.BlockSpec((TM, D_MODEL), row),
        out_shape=jax.ShapeDtypeStruct((n, D_MODEL), F32),
        scratch_shapes=[pltpu.VMEM((DIL_W // LANES, TM, LANES), F32)] * 4,
        compiler_params=_params("parallel"),
        name="outproj_even",
    )(oa, *pat_args, wa, wb, x, g, b)


def _outproj_odd_kernel(o_ref, w_ref, x_ref, g_ref, b_ref, y_ref):
    mix = jnp.dot(o_ref[...], w_ref[...], preferred_element_type=F32)
    y_ref[...] = _layer_norm(DN_ALPHA * x_ref[...] + mix, g_ref[...], b_ref[...])


def _outproj_odd(o, w, x, g, b):
    n = x.shape[0]
    row = lambda i: (i, 0)
    return pl.pallas_call(
        _outproj_odd_kernel,
        grid=(n // TM,),
        in_specs=[pl.BlockSpec((TM, o.shape[1]), row), _const_spec(w.shape),
                  pl.BlockSpec((TM, D_MODEL), row), _const_spec(g.shape), _const_spec(b.shape)],
        out_specs=pl.BlockSpec((TM, D_MODEL), row),
        out_shape=jax.ShapeDtypeStruct((n, D_MODEL), F32),
        compiler_params=_params("parallel"),
        name="outproj_odd",
    )(o, w, x, g, b)


def _inproj_odd_kernel(x_ref, wq_ref, wk_ref, wv_ref, tb_ref, q_ref, k_ref, v_ref):
    xb = x_ref[...].astype(BF16)
    dot = functools.partial(jnp.dot, preferred_element_type=F32)
    q12 = dot(xb, wq_ref[...])
    q_ref[...] = (q12[:, :1024] * _tile_lanes(tb_ref[0], 8)
                  + q12[:, 1024:] * _tile_lanes(tb_ref[1], 8)).astype(BF16)
    k12 = dot(xb, wk_ref[...])
    k_ref[...] = (k12[:, :512] * _tile_lanes(tb_ref[2], 4)
                  + k12[:, 512:] * _tile_lanes(tb_ref[3], 4)).astype(BF16)
    v_ref[...] = dot(xb, wv_ref[...]).astype(BF16)


def _inproj_odd(x, w, tb, seq):
    n = x.shape[0]
    pos_blocks = seq // TM
    row = lambda i: (i, 0)
    ws = [w["wq"], w["wk"], w["wv"]]
    out_w = (1024, 512, 512)
    return pl.pallas_call(
        _inproj_odd_kernel,
        grid=(n // TM,),
        in_specs=[pl.BlockSpec((TM, D_MODEL), row)] + [_const_spec(a.shape) for a in ws]
        + [pl.BlockSpec((4, TM, LANES), lambda i: (0, i % pos_blocks, 0))],
        out_specs=[pl.BlockSpec((TM, c), row) for c in out_w],
        out_shape=[jax.ShapeDtypeStruct((n, c), BF16) for c in out_w],
        compiler_params=_params("parallel"),
        name="inproj_odd",
    )(x, *ws, tb)


def _sorting_network(n):
    pairs = []

    def merge(lo, hi, r):
        step = r * 2
        if step < hi - lo:
            merge(lo, hi, step)
            merge(lo + r, hi, step)
            pairs.extend((i, i + r) for i in range(lo + r, hi - r, step))
        else:
            pairs.append((lo, lo + r))

    def sort(lo, hi):
        if hi - lo >= 1:
            mid = lo + (hi - lo) // 2
            sort(lo, mid)
            sort(mid + 1, hi)
            merge(lo, hi, 1)

    sort(0, n - 1)
    return pairs


def _top16_sorted(s):
    k = PEER_TOPK
    w = [s[g * SUBLANES:(g + 1) * SUBLANES, :] for g in range(s.shape[0] // SUBLANES)]
    w += [jnp.full_like(w[0], NEG)] * (k - len(w))

    def exchange(i, j):
        w[i], w[j] = jnp.maximum(w[i], w[j]), jnp.minimum(w[i], w[j])

    for i, j in _sorting_network(k):
        exchange(i, j)
    shift = SUBLANES // 2
    while shift >= 1:
        other = [pltpu.roll(t, shift, 0) for t in w]
        w = [jnp.maximum(w[p], other[k - 1 - p]) for p in range(k)]
        d = k // 2
        while d >= 1:
            for i in range(k):
                if i & d == 0:
                    exchange(i, i + d)
            d //= 2
        shift //= 2
    return w


def _split_bf16(t):
    hi = t.astype(BF16)
    return hi, (t - hi.astype(F32)).astype(BF16)


def _dot3(a_hi, a_lo, b_hi, b_lo, dims):
    d = functools.partial(lax.dot_general, dimension_numbers=dims, preferred_element_type=F32)
    return d(a_hi, b_hi) + (d(a_hi, b_lo) + d(a_lo, b_hi))


def _peer_select_kernel(x_ref, wqh_ref, wql_ref, kh_ref, kl_ref, e2_ref, thr_ref, e1_ref,
                        qh_ref, ql_ref, a_ref, b_ref):
    h = pl.program_id(1)

    @pl.when(h == 0)
    def _project():
        xh, xl = _split_bf16(x_ref[...])
        q = _dot3(wqh_ref[...], wql_ref[...], xh, xl, _NT)
        qh_ref[...], ql_ref[...] = _split_bf16(q)

    half = PEER_D_KEY // 2
    row0 = pl.multiple_of(h * PEER_D_KEY, PEER_D_KEY)
    mm = (((1,), (0,)), ((), ()))
    s1 = _dot3(kh_ref[0, 0], kl_ref[0, 0], qh_ref[pl.ds(row0, half), :], ql_ref[pl.ds(row0, half), :], mm)
    s2 = _dot3(kh_ref[0, 1], kl_ref[0, 1], qh_ref[pl.ds(row0 + half, half), :],
               ql_ref[pl.ds(row0 + half, half), :], mm)
    for lt in range(TM_SEL // LANES):
        ls = slice(lt * LANES, (lt + 1) * LANES)
        s1t, s2t = s1[:, ls], s2[:, ls]
        a_rep, b_rep = _top16_sorted(s1t), _top16_sorted(s2t)
        for p in range(PEER_TOPK):
            a_ref[p:p + 1, ls] = a_rep[p][0:1]
            b_ref[p:p + 1, ls] = b_rep[p][0:1]
        a16, b16 = a_ref[:, ls], b_ref[:, ls]
        a0, b0 = a16[0:1], b16[0:1]
        cand = [a0 + b16]
        cand += [a16[i:i + 1] + b16[0:8] for i in range(1, 8)]
        cand += [a16[8:16] + b0]
        cand = jnp.concatenate(cand, axis=0)
        v16 = _top16_sorted(cand)[PEER_TOPK - 1][0:1]
        z = jnp.sum(jnp.where(cand >= v16, jnp.exp(cand - (a0 + b0)), 0.0), axis=0, keepdims=True)
        half_inv_z = 0.5 / z
        e2_sorted = jnp.exp(b16 - b0) * half_inv_z
        thr = jnp.full(s1t.shape, NO_KEY_QUALIFIES, F32)
        for i in range(PEER_TOPK):
            t_i = jnp.min(jnp.where(a16[i:i + 1] + b16 >= v16, e2_sorted, NO_KEY_QUALIFIES), axis=0, keepdims=True)
            thr = jnp.where(s1t == _tile_rows(a_rep[i], PEER_N_KEYS // SUBLANES), t_i, thr)
        e2_ref[:, ls] = jnp.exp(s2t - b0) * half_inv_z
        thr_ref[:, ls] = thr
        e1_ref[:, ls] = jnp.exp(s1t - a0)


def _peer_select(x, wq_hi, wq_lo, keys_hi, keys_lo):
    n = x.shape[0]
    spec = pl.BlockSpec((PEER_N_KEYS, TM_SEL), lambda i, h: (h, i))
    shape = lambda dt: jax.ShapeDtypeStruct((PEER_HEADS * PEER_N_KEYS, n), dt)
    kspec = pl.BlockSpec((1, 2, PEER_N_KEYS, PEER_D_KEY // 2), lambda i, h: (h, 0, 0, 0))
    return pl.pallas_call(
        _peer_select_kernel,
        grid=(n // TM_SEL, PEER_HEADS),
        in_specs=[pl.BlockSpec((TM_SEL, D_MODEL), lambda i, h: (i, 0)),
                  _const_spec(wq_hi.shape), _const_spec(wq_lo.shape), kspec, kspec],
        out_specs=[spec] * 3,
        out_shape=[shape(F32)] * 3,
        scratch_shapes=[pltpu.VMEM((PEER_HEADS * PEER_D_KEY, TM_SEL), BF16),
                        pltpu.VMEM((PEER_HEADS * PEER_D_KEY, TM_SEL), BF16),
                        pltpu.VMEM((PEER_TOPK, TM_SEL), F32),
                        pltpu.VMEM((PEER_TOPK, TM_SEL), F32)],
        compiler_params=_params("parallel", "arbitrary"),
        name="peer_select",
    )(x, wq_hi, wq_lo, keys_hi, keys_lo)


def _peer_expert_kernel(x_ref, e2_ref, thr_ref, e1_ref, u_ref, vt_ref, g_ref, b_ref, y_ref,
                        xb_ref, a_ref, h_ref, acc_ref):
    eb = pl.program_id(1)
    n_i1 = EB_PEER // PEER_N_KEYS
    halves = 2
    rows_half = EB_PEER // halves
    n_r = 1
    slab = PEER_N_KEYS // 1

    @pl.when(eb == 0)
    def _init():
        xb_ref[...] = x_ref[...].astype(BF16)
        acc_ref[...] = jnp.zeros(acc_ref.shape, F32)

    for hh in range(halves):
        rs = slice(hh * rows_half, (hh + 1) * rows_half)
        a_ref[rs, :] = lax.dot_general(u_ref[rs, :], xb_ref[...], _NT, preferred_element_type=F32)
    i1_base = pl.multiple_of(eb * n_i1, SUBLANES)
    for hh in range(halves):
        for lt in range(TM_PEER // LANES):
            ls = slice(lt * LANES, (lt + 1) * LANES)
            for rp in range(n_i1 // halves // n_r):
                r0 = hh * (n_i1 // halves) + n_r * rp
                for sb in range(PEER_N_KEYS // slab):
                    gate = [jnp.zeros((slab, LANES), F32)] * n_r
                    for h in range(PEER_HEADS):
                        t8 = thr_ref[pl.ds(h * PEER_N_KEYS + i1_base, n_i1), ls]
                        e8 = e1_ref[pl.ds(h * PEER_N_KEYS + i1_base, n_i1), ls]
                        e2t = e2_ref[h * PEER_N_KEYS + sb * slab:h * PEER_N_KEYS + (sb + 1) * slab, ls]
                        for t in range(n_r):
                            r = r0 + t
                            gate[t] = gate[t] + jnp.where(e2t >= t8[r:r + 1], e2t, 0.0) * e8[r:r + 1]
                    for t in range(n_r):
                        rows = slice((r0 + t) * PEER_N_KEYS + sb * slab, (r0 + t) * PEER_N_KEYS + (sb + 1) * slab)
                        a = a_ref[rows, ls]
                        h_ref[rows, ls] = (a * (1.0 + lax.erf(a * INV_SQRT2)) * gate[t]).astype(BF16)
        rs = slice(hh * rows_half, (hh + 1) * rows_half)
        acc_ref[...] += jnp.dot(vt_ref[:, rs], h_ref[rs, :], preferred_element_type=F32)

    @pl.when(eb == pl.num_programs(1) - 1)
    def _fin():
        y = DN_ALPHA * x_ref[...] + acc_ref[...].T
        y_ref[...] = _layer_norm(y, g_ref[...], b_ref[...])


def _peer_experts(x, stats, u_bf, vt_bf, layer, g, b):
    n = x.shape[0]
    stat_spec = pl.BlockSpec((PEER_HEADS * PEER_N_KEYS, TM_PEER), lambda i, e: (0, i))
    return pl.pallas_call(
        _peer_expert_kernel,
        grid=(n // TM_PEER, PEER_N_EXPERTS // EB_PEER),
        in_specs=[pl.BlockSpec((TM_PEER, D_MODEL), lambda i, e: (i, 0))] + [stat_spec] * 3
        + [pl.BlockSpec((None, EB_PEER, D_MODEL), lambda i, e: (layer, e, 0)),
           pl.BlockSpec((None, D_MODEL, EB_PEER), lambda i, e: (layer, 0, e)),
           _const_spec(g.shape), _const_spec(b.shape)],
        out_specs=pl.BlockSpec((TM_PEER, D_MODEL), lambda i, e: (i, 0)),
        out_shape=jax.ShapeDtypeStruct((n, D_MODEL), F32),
        scratch_shapes=[pltpu.VMEM((TM_PEER, D_MODEL), BF16),
                        pltpu.VMEM((EB_PEER, TM_PEER), F32),
                        pltpu.VMEM((EB_PEER, TM_PEER), BF16),
                        pltpu.VMEM((D_MODEL, TM_PEER), F32)],
        compiler_params=_params("parallel", "arbitrary"),
        name="peer_experts",
    )(x, *stats, u_bf, vt_bf, g, b)


def _ple_kernel(x_ref, p_ref, wp_ref, wg_ref, y_ref):
    x = x_ref[...]
    proj = jnp.dot(p_ref[...].astype(BF16), wp_ref[...], preferred_element_type=F32)
    gate = jax.nn.sigmoid(jnp.dot(x.astype(BF16), wg_ref[...], preferred_element_type=F32))
    y_ref[...] = x + proj * gate


def _ple(x, p, layer, wp, wg):
    n = x.shape[0]
    row = lambda i: (i, 0)
    return pl.pallas_call(
        _ple_kernel,
        grid=(n // TM,),
        in_specs=[pl.BlockSpec((TM, D_MODEL), row), pl.BlockSpec((None, TM, PLE_DIM), lambda i: (layer, i, 0)),
                  _const_spec(wp.shape), _const_spec(wg.shape)],
        out_specs=pl.BlockSpec((TM, D_MODEL), row),
        out_shape=jax.ShapeDtypeStruct((n, D_MODEL), F32),
        compiler_params=_params("parallel"),
        name="ple_gate",
    )(x, p, wp, wg)


def _rope_partner_cols(w, head_dim, half):
    n_heads = w.shape[1] // head_dim
    w3 = w.reshape(w.shape[0], n_heads, head_dim)
    sw = jnp.concatenate([w3[..., half:2 * half], w3[..., :half],
                          jnp.zeros_like(w3[..., 2 * half:])], axis=-1)
    return sw.reshape(w.shape)


def _twice_per_head(w, head_dim):
    n_heads = w.shape[1] // head_dim
    w3 = w.reshape(w.shape[0], n_heads, head_dim)
    return jnp.concatenate([w3, w3], axis=-1).reshape(w.shape[0], 2 * w.shape[1])


def _rope_tables(seq, half, theta, lanes_before, lanes_after, pass_through, scale):
    inv = theta ** (-jnp.arange(half, dtype=F32) / half)
    ang = jnp.arange(seq).astype(F32)[:, None] * inv[None, :]
    cos, sin = jnp.cos(ang), jnp.sin(ang)
    ones = lambda w, v: jnp.full((seq, w), v, F32)
    ct = jnp.concatenate([ones(lanes_before, pass_through), cos, cos, ones(lanes_after, pass_through)], axis=-1)
    st = jnp.concatenate([ones(lanes_before, 0.0), -sin, sin, ones(lanes_after, 0.0)], axis=-1)
    return ct * scale, st * scale


def _prep_even(a_w_in, a_q_norm, a_kv_norm, a_w_uq, a_w_ukv, a_w_out):
    o1 = MLA_Q_LORA
    o2 = o1 + MLA_KV_LORA
    o3 = o2 + MLA_ROPE
    nb = DIL_HEADS * DIL_HD
    bf = lambda t: t.astype(BF16)
    w = {}
    w["wcq"] = bf(a_w_in[:, :o1])
    w["wckv"] = bf(a_w_in[:, o1:o2])
    kr = a_w_in[:, o2:o3]
    krs = jnp.concatenate([kr[:, MLA_ROPE // 2:], kr[:, :MLA_ROPE // 2]], axis=-1)
    pad_slot = lambda t: jnp.pad(t, ((0, 0), (MLA_NOPE, MLA_SLOT - MLA_NOPE - MLA_ROPE)))
    w["wkr"] = bf(jnp.concatenate([pad_slot(kr), pad_slot(krs)], axis=-1))
    qb = a_w_in[:, o3:o3 + nb]
    kb = a_w_in[:, o3 + nb:o3 + 2 * nb]
    rd2 = DIL_HD // ROPE_FRACTION // 2
    w["wqb"] = bf(jnp.concatenate([qb, _rope_partner_cols(qb, DIL_HD, rd2)], axis=-1))
    w["wkb"] = bf(jnp.concatenate([kb, _rope_partner_cols(kb, DIL_HD, rd2)], axis=-1))
    w["wvb"] = bf(a_w_in[:, o3 + 2 * nb:])
    w["qn"] = a_q_norm.reshape(1, -1)
    w["kvn"] = a_kv_norm.reshape(1, -1)
    uq = a_w_uq.reshape(MLA_Q_LORA, MLA_HEADS, MLA_NOPE + MLA_ROPE)
    zpad = jnp.zeros((MLA_Q_LORA, MLA_HEADS, MLA_SLOT - MLA_NOPE - MLA_ROPE), F32)
    uq1 = jnp.concatenate([uq, zpad], axis=-1).reshape(MLA_Q_LORA, MLA_HEADS * MLA_SLOT)
    r = uq[..., MLA_NOPE:]
    uq2 = jnp.concatenate([jnp.zeros_like(uq[..., :MLA_NOPE]), r[..., MLA_ROPE // 2:], r[..., :MLA_ROPE // 2],
                           zpad], axis=-1).reshape(MLA_Q_LORA, MLA_HEADS * MLA_SLOT)
    w["wuq"] = bf(jnp.concatenate([uq1, uq2], axis=-1))
    ukv = a_w_ukv.reshape(MLA_KV_LORA, MLA_HEADS, MLA_NOPE + MLA_V)
    kpad = jnp.zeros((MLA_KV_LORA, MLA_HEADS, MLA_SLOT - MLA_NOPE), F32)
    w["wk"] = bf(jnp.concatenate([ukv[..., :MLA_NOPE], kpad], axis=-1).reshape(MLA_KV_LORA, -1))
    w["wv"] = bf(ukv[..., MLA_NOPE:].reshape(MLA_KV_LORA, MLA_HEADS * MLA_V))
    w["wout_a"] = bf(a_w_out[:MLA_HEADS * MLA_V])
    w["wout_b"] = bf(a_w_out[MLA_HEADS * MLA_V:])
    return w


def _prep_odd(c_w_in, c_w_out):
    e1 = C_HEADS * C_HD
    e2 = e1 + C_KV_HEADS * C_HD
    rd2 = C_HD // ROPE_FRACTION // 2
    q, k = c_w_in[:, :e1], c_w_in[:, e1:e2]
    k2 = lambda t: _twice_per_head(t, C_HD)
    return {
        "wq": jnp.concatenate([q, _rope_partner_cols(q, C_HD, rd2)], axis=-1).astype(BF16),
        "wk": jnp.concatenate([k2(k), k2(_rope_partner_cols(k, C_HD, rd2))], axis=-1).astype(BF16),
        "wv": k2(c_w_in[:, e2:]).astype(BF16),
        "wout": c_w_out.astype(BF16),
    }


def _split_kernel(w_ref, hi_ref, lo_ref):
    hi_ref[...], lo_ref[...] = _split_bf16(w_ref[...])


def _split_hi_lo(w):
    flat = w.reshape(-1, LANES)
    rows = flat.shape[0]
    blk = min(rows, 1024)
    spec = pl.BlockSpec((blk, LANES), lambda i: (i, 0))
    hi, lo = pl.pallas_call(
        _split_kernel,
        grid=(rows // blk,),
        in_specs=[spec], out_specs=[spec, spec],
        out_shape=[jax.ShapeDtypeStruct(flat.shape, BF16)] * 2,
        compiler_params=_params("parallel"),
        name="split_hi_lo",
    )(flat)
    return hi.reshape(w.shape), lo.reshape(w.shape)


def _prep_peer(peer_w_q, peer_sub_keys, peer_u, peer_v):
    wq_hi, wq_lo = _split_hi_lo(jnp.swapaxes(peer_w_q, 1, 2))
    keys_hi, keys_lo = _split_hi_lo(peer_sub_keys)
    return {
        "wq_hi": wq_hi, "wq_lo": wq_lo, "keys_hi": keys_hi, "keys_lo": keys_lo,
        "u": peer_u.astype(BF16), "vt": jnp.swapaxes(peer_v, 1, 2).astype(BF16),
    }


def _tables_mla(seq):
    scale = (MLA_NOPE + MLA_ROPE) ** -0.5
    pad = MLA_SLOT - MLA_NOPE - MLA_ROPE
    cq, sq = _rope_tables(seq, MLA_ROPE // 2, MLA_THETA, MLA_NOPE, pad, 1.0, scale)
    ck, sk = _rope_tables(seq, MLA_ROPE // 2, MLA_THETA, MLA_NOPE, pad, 0.0, 1.0)
    return jnp.stack([cq, sq, ck, sk])


def _tables_partial(seq, head_dim, q_scale):
    half = head_dim // ROPE_FRACTION // 2
    cq, sq = _rope_tables(seq, half, ROPE_THETA, 0, head_dim - 2 * half, 1.0, q_scale)
    ck, sk = _rope_tables(seq, half, ROPE_THETA, 0, head_dim - 2 * half, 1.0, 1.0)
    two = lambda t: jnp.concatenate([t, t], axis=-1)
    return jnp.stack([two(cq), two(sq), two(ck), two(sk)])


def _peer_layer(x, pw, i, g, b):
    stats = _peer_select(x, pw["wq_hi"][i], pw["wq_lo"][i], pw["keys_hi"][i], pw["keys_lo"][i])
    return _peer_experts(x, stats, pw["u"], pw["vt"], i, g, b)


def _mixer_even(x, w, n_seq, seq, g, b):
    outs = _inproj_even(x, w, _tables_mla(seq), _tables_partial(seq, DIL_HD, DIL_HD ** -0.5), seq)
    qa, ka, va = outs[:3]
    oa = _mla_attention(qa, ka, va, n_seq, seq)
    pats = []
    for p, (window, dil) in enumerate(DIL_PATTERNS):
        qd, kd, vd = outs[3 + 3 * p:6 + 3 * p]
        pats.append(_banded_attention(qd, kd, vd, n_seq, seq, dil, (window // 2) // dil, lambda m: m, emit_lse=True))
    return _outproj_even(oa, pats, w["wout_a"], w["wout_b"], x, g, b)


def _mixer_odd(x, w, sink, n_seq, seq, g, b):
    q, k, v = _inproj_odd(x, w, _tables_partial(seq, C_HD, C_HD ** -0.5), seq)
    pairs_per_group = C_HEADS // C_KV_HEADS // 2
    (o,) = _banded_attention(q, k, v, n_seq, seq, 1, C_RADIUS, lambda m: m // pairs_per_group, sink=sink)
    return _outproj_odd(o, w["wout"], x, g, b)


def _trunk(x3, p4, prm):
    n_seq, seq, _ = x3.shape
    x = x3.reshape(n_seq * seq, D_MODEL)
    p = p4.reshape(DEPTH, n_seq * seq, PLE_DIM)
    row = lambda t: t.reshape(1, -1)
    for i in range(DEPTH):
        g, b = row(prm["ln_mix_g"][i]), row(prm["ln_mix_b"][i])
        if i % 2 == 0:
            x = _mixer_even(x, prm["even"], n_seq, seq, g, b)
        else:
            x = _mixer_odd(x, prm["odd"], prm["sink"], n_seq, seq, g, b)
        x = _peer_layer(x, prm["peer"], i, row(prm["ln_ffn_g"][i]), row(prm["ln_ffn_b"][i]))
        x = _ple(x, p, i, prm["ple_proj"][i], prm["ple_gate"][i])
    return x.reshape(n_seq, seq, D_MODEL)


def kernel(x_prompt, x_sample, p_prompt, p_sample, a_w_in, a_q_norm, a_kv_norm, a_w_uq, a_w_ukv, a_w_out,
           c_w_in, c_sink, c_w_out, ln_mix_g, ln_mix_b, ln_ffn_g, ln_ffn_b, peer_w_q, peer_sub_keys,
           peer_u, peer_v, ple_proj, ple_gate):
    prm = {
        "even": _prep_even(a_w_in[0], a_q_norm[0], a_kv_norm[0], a_w_uq[0], a_w_ukv[0], a_w_out[0]),
        "odd": _prep_odd(c_w_in[0], c_w_out[0]),
        "sink": c_sink[0],
        "ln_mix_g": ln_mix_g, "ln_mix_b": ln_mix_b, "ln_ffn_g": ln_ffn_g, "ln_ffn_b": ln_ffn_b,
        "peer": _prep_peer(peer_w_q, peer_sub_keys, peer_u, peer_v),
        "ple_proj": ple_proj.astype(BF16),
        "ple_gate": ple_gate.astype(BF16),
    }
    y_prompt = _trunk(x_prompt, p_prompt, prm)
    y_sample = _trunk(x_sample, p_sample, prm)
    return (y_prompt, y_sample)
```
